```python
import math
import jax, jax.numpy as jnp
from jax import lax
import numpy as np

D_MODEL = 1024
BATCH = 8
SEQ = 2048
DEPTH = 1

PLE_DIM = 256
DA_HEADS = 8
DA_HEAD_DIM = 64
DA_V_DIM = 2 * DA_HEAD_DIM
DA_QK_WIDTH = DA_HEADS * 2 * DA_HEAD_DIM
DA_V_WIDTH = DA_HEADS * DA_V_DIM
GDN_QK_HEADS = 8
GDN_V_HEADS = 16
GDN_K_DIM = 128
GDN_V_DIM = 128
GDN_QK_WIDTH = GDN_QK_HEADS * GDN_K_DIM
GDN_V_WIDTH = GDN_V_HEADS * GDN_V_DIM
GDN_CONV_CH = 2 * GDN_QK_WIDTH + GDN_V_WIDTH
CONV_WIDTH = 4
CHUNK = 64
Q_BLOCK = 128
D_FF = 4 * D_MODEL
EPS = 1e-6
IN_SIZES = (DA_QK_WIDTH, DA_QK_WIDTH, DA_V_WIDTH, GDN_CONV_CH, GDN_V_WIDTH, GDN_V_HEADS, GDN_V_HEADS, D_MODEL, D_MODEL)
IN_WIDTH = sum(IN_SIZES)

kernel_name = 'hybrid_diffattn_gdn_block'


def _split_points():
    pts = []
    acc = 0
    for s in IN_SIZES[:-1]:
        acc += s
        pts.append(acc)
    return pts


def rmsnorm(x, gain):
    x32 = x.astype(jnp.float32)
    y = x32 * lax.rsqrt(jnp.mean(x32 * x32, axis=-1, keepdims=True) + EPS)
    return y.astype(x.dtype) * gain


def l2norm(x):
    x32 = x.astype(jnp.float32)
    return x32 * lax.rsqrt(jnp.sum(x32 * x32, axis=-1, keepdims=True) + EPS)


def causal_depthwise_conv(x, w):
    k_w, c = w.shape
    return lax.conv_general_dilated(x, w[:, None, :].astype(x.dtype), window_strides=(1,),
                                    padding=[(k_w - 1, 0)], dimension_numbers=('NWC', 'WIO', 'NWC'),
                                    feature_group_count=c)


def diff_attention(q, k, v, lq1, lk1, lq2, lk2, sub_gain, lambda_init):
    b, t = q.shape[0], q.shape[1]
    nb = t // Q_BLOCK
    lam = (jnp.exp(jnp.sum(lq1.astype(jnp.float32) * lk1.astype(jnp.float32)))
           - jnp.exp(jnp.sum(lq2.astype(jnp.float32) * lk2.astype(jnp.float32))) + lambda_init)
    qb = (q * DA_HEAD_DIM ** -0.5).reshape(b, nb, Q_BLOCK, 2 * DA_HEADS, DA_HEAD_DIM).swapaxes(0, 1)
    kpos = jnp.arange(t)

    def one_block(args):
        q_blk, blk = args
        s = jnp.einsum('bqhd,bkhd->bhqk', q_blk, k).astype(jnp.float32)
        qpos = blk * Q_BLOCK + jnp.arange(Q_BLOCK)
        s = jnp.where(kpos[None, :] <= qpos[:, None], s, -jnp.inf)
        w = jax.nn.softmax(s, axis=-1).reshape(b, DA_HEADS, 2, Q_BLOCK, t)
        a = w[:, :, 0] - lam * w[:, :, 1]
        return jnp.einsum('bhqk,bkhd->bqhd', a.astype(v.dtype), v)

    o = lax.map(one_block, (qb, jnp.arange(nb)))
    o = o.swapaxes(0, 1).reshape(b, t, DA_HEADS, DA_V_DIM)
    o = rmsnorm(o, sub_gain) * (1.0 - lambda_init)
    return o.reshape(b, t, DA_V_WIDTH)


def gated_delta_rule(q, k, v, g, beta):
    b, t, h, dk = q.shape
    dv = v.shape[-1]
    n = t // CHUNK

    def to_chunks(z):
        z = z.reshape((b, n, CHUNK, h) + z.shape[3:])
        return jnp.moveaxis(z, 3, 1)

    q = to_chunks(q * dk ** -0.5)
    k = to_chunks(k)
    v = to_chunks(v)
    g = to_chunks(g)
    beta = to_chunks(beta)
    gc = jnp.cumsum(g, axis=-1)
    causal = jnp.tril(jnp.ones((CHUNK, CHUNK), dtype=bool))
    decay = jnp.exp(jnp.where(causal, gc[..., :, None] - gc[..., None, :], -jnp.inf))
    k_beta = k * beta[..., None]
    v_beta = v * beta[..., None]
    lower = jnp.tril(jnp.einsum('bhncd,bhnsd->bhncs', k_beta, k) * decay, -1)
    eye = jnp.eye(CHUNK, dtype=jnp.float32)
    tmat = lax.linalg.triangular_solve(eye + lower, jnp.broadcast_to(eye, lower.shape),
                                       left_side=True, lower=True, unit_diagonal=True)
    u = jnp.einsum('bhncs,bhnse->bhnce', tmat, v_beta)
    w = jnp.einsum('bhncs,bhnsd->bhncd', tmat, k_beta * jnp.exp(gc)[..., None])
    intra = jnp.einsum('bhncd,bhnsd->bhncs', q, k) * decay

    def step(state, xs):
        q_c, k_c, u_c, w_c, gc_c, intra_c = xs
        v_new = u_c - jnp.einsum('bhcd,bhde->bhce', w_c, state)
        o_c = (jnp.einsum('bhcd,bhde->bhce', q_c * jnp.exp(gc_c)[..., None], state)
               + jnp.einsum('bhcs,bhse->bhce', intra_c, v_new))
        g_last = gc_c[..., -1]
        k_dec = k_c * jnp.exp(g_last[..., None] - gc_c)[..., None]
        state = state * jnp.exp(g_last)[..., None, None] + jnp.einsum('bhcd,bhce->bhde', k_dec, v_new)
        return state, o_c

    xs = tuple(jnp.moveaxis(z, 2, 0) for z in (q, k, u, w, gc, intra))
    state0 = jnp.zeros((b, h, dk, dv), jnp.float32)
    _, o = lax.scan(step, state0, xs)
    return jnp.transpose(o, (1, 0, 3, 2, 4)).reshape(b, t, h, dv)


def setup_inputs(seed: int = 0) -> dict:
    key = jax.random.key(seed)
    ks = jax.random.split(key, 32)

    def dense(k, fan_in, shape):
        return jax.random.normal(k, shape, jnp.float32) * fan_in ** -0.5

    def gain(k, n):
        return 1.0 + 0.02 * jax.random.normal(k, (DEPTH, n), jnp.float32)

    x = jax.random.normal(ks[0], (BATCH, SEQ, D_MODEL), jnp.float32)
    p = jax.random.normal(ks[1], (DEPTH, BATCH, SEQ, PLE_DIM), jnp.float32)
    pre_mix_norm = gain(ks[2], D_MODEL)
    w_in = dense(ks[3], D_MODEL, (DEPTH, D_MODEL, IN_WIDTH))
    conv_w = dense(ks[4], CONV_WIDTH, (DEPTH, CONV_WIDTH, GDN_CONV_CH))
    lambda_q1 = 0.1 * jax.random.normal(ks[5], (DEPTH, DA_HEAD_DIM), jnp.float32)
    lambda_k1 = 0.1 * jax.random.normal(ks[6], (DEPTH, DA_HEAD_DIM), jnp.float32)
    lambda_q2 = 0.1 * jax.random.normal(ks[7], (DEPTH, DA_HEAD_DIM), jnp.float32)
    lambda_k2 = 0.1 * jax.random.normal(ks[8], (DEPTH, DA_HEAD_DIM), jnp.float32)
    da_sub_norm = gain(ks[9], DA_V_DIM)
    gdn_a_log = jnp.log(jax.random.uniform(ks[10], (DEPTH, GDN_V_HEADS), jnp.float32, 1.0, 16.0))
    dt = jnp.exp(jax.random.uniform(ks[11], (DEPTH, GDN_V_HEADS), jnp.float32,
                                    math.log(1e-3), math.log(0.1)))
    gdn_dt_bias = dt + jnp.log(-jnp.expm1(-dt))
    gdn_out_norm = gain(ks[12], GDN_V_DIM)
    w_branch_a = dense(ks[13], DA_V_WIDTH, (DEPTH, DA_V_WIDTH, D_MODEL))
    w_branch_b = dense(ks[14], GDN_V_WIDTH, (DEPTH, GDN_V_WIDTH, D_MODEL))
    w_out = dense(ks[15], D_MODEL, (DEPTH, D_MODEL, D_MODEL))
    post_mix_norm = gain(ks[16], D_MODEL)
    pre_mlp_norm = gain(ks[17], D_MODEL)
    w_up = dense(ks[18], D_MODEL, (DEPTH, D_MODEL, D_FF))
    w_down = dense(ks[19], D_FF, (DEPTH, D_FF, D_MODEL))
    post_mlp_norm = gain(ks[20], D_MODEL)
    w_ple = dense(ks[21], PLE_DIM, (DEPTH, PLE_DIM, D_MODEL))
    w_ple_gate = dense(ks[22], D_MODEL, (DEPTH, D_MODEL, D_MODEL))
    ple_norm = gain(ks[23], D_MODEL)
    return {'x': x, 'p': p, 'pre_mix_norm': pre_mix_norm, 'w_in': w_in, 'conv_w': conv_w,
            'lambda_q1': lambda_q1, 'lambda_k1': lambda_k1, 'lambda_q2': lambda_q2, 'lambda_k2': lambda_k2,
            'da_sub_norm': da_sub_norm, 'gdn_a_log': gdn_a_log, 'gdn_dt_bias': gdn_dt_bias,
            'gdn_out_norm': gdn_out_norm, 'w_branch_a': w_branch_a, 'w_branch_b': w_branch_b,
            'w_out': w_out, 'post_mix_norm': post_mix_norm, 'pre_mlp_norm': pre_mlp_norm,
            'w_up': w_up, 'w_down': w_down, 'post_mlp_norm': post_mlp_norm,
            'w_ple': w_ple, 'w_ple_gate': w_ple_gate, 'ple_norm': ple_norm}


def reference(x, p, pre_mix_norm, w_in, conv_w, lambda_q1, lambda_k1, lambda_q2, lambda_k2,
              da_sub_norm, gdn_a_log, gdn_dt_bias, gdn_out_norm, w_branch_a, w_branch_b, w_out,
              post_mix_norm, pre_mlp_norm, w_up, w_down, post_mlp_norm, w_ple, w_ple_gate, ple_norm):
    b, t, _ = x.shape
    splits = _split_points()
    rep = GDN_V_HEADS // GDN_QK_HEADS
    h = x
    for i in range(DEPTH):
        lambda_init = 0.8 - 0.6 * math.exp(-0.3 * i)
        u = rmsnorm(h, pre_mix_norm[i])
        proj = jnp.einsum('btd,de->bte', u, w_in[i])
        (da_q, da_k, da_v, gdn_qkv, gdn_z, gdn_b, gdn_a,
         gate_a_in, gate_b_in) = jnp.split(proj, splits, axis=-1)
        o_a = diff_attention(da_q.reshape(b, t, 2 * DA_HEADS, DA_HEAD_DIM),
                             da_k.reshape(b, t, 2 * DA_HEADS, DA_HEAD_DIM),
                             da_v.reshape(b, t, DA_HEADS, DA_V_DIM),
                             lambda_q1[i], lambda_k1[i], lambda_q2[i], lambda_k2[i],
                             da_sub_norm[i], lambda_init)
        qkv = jax.nn.silu(causal_depthwise_conv(gdn_qkv, conv_w[i]))
        g_q, g_k, g_v = jnp.split(qkv, [GDN_QK_WIDTH, 2 * GDN_QK_WIDTH], axis=-1)
        g_q = jnp.repeat(l2norm(g_q.reshape(b, t, GDN_QK_HEADS, GDN_K_DIM)), rep, axis=2)
        g_k = jnp.repeat(l2norm(g_k.reshape(b, t, GDN_QK_HEADS, GDN_K_DIM)), rep, axis=2)
        g_v = g_v.reshape(b, t, GDN_V_HEADS, GDN_V_DIM).astype(jnp.float32)
        beta = jax.nn.sigmoid(gdn_b.astype(jnp.float32))
        log_decay = -jnp.exp(gdn_a_log[i].astype(jnp.float32)) * jax.nn.softplus(
            gdn_a.astype(jnp.float32) + gdn_dt_bias[i].astype(jnp.float32))
        o_b = gated_delta_rule(g_q, g_k, g_v, log_decay, beta)
        o_b = rmsnorm(o_b, gdn_out_norm[i].astype(jnp.float32)) * jax.nn.silu(
            gdn_z.reshape(b, t, GDN_V_HEADS, GDN_V_DIM).astype(jnp.float32))
        o_b = o_b.reshape(b, t, GDN_V_WIDTH).astype(x.dtype)
        merged = (jax.nn.sigmoid(gate_a_in) * jnp.einsum('bte,ed->btd', o_a, w_branch_a[i])
                  + jax.nn.sigmoid(gate_b_in) * jnp.einsum('bte,ed->btd', o_b, w_branch_b[i]))
        mix = jnp.einsum('btd,de->bte', merged, w_out[i])
        h = h + rmsnorm(mix, post_mix_norm[i])
        u = rmsnorm(h, pre_mlp_norm[i])
        hid = jnp.square(jax.nn.relu(jnp.einsum('btd,df->btf', u, w_up[i])))
        h = h + rmsnorm(jnp.einsum('btf,fd->btd', hid, w_down[i]), post_mlp_norm[i])
        e = jnp.einsum('btp,pd->btd', p[i], w_ple[i]) * jax.nn.sigmoid(
            jnp.einsum('btd,de->bte', h, w_ple_gate[i]))
        h = h + rmsnorm(e, ple_norm[i])
    return h
```

```python
import functools
import math

import jax
import jax.numpy as jnp
from jax import lax
from jax.experimental import pallas as pl
from jax.experimental.pallas import tpu as pltpu

F32 = jnp.float32
BF16 = jnp.bfloat16

EPS = 1e-6
LANES = 128
DA_HEADS = 8
DA_HEAD_DIM = 64
GDN_QK_HEADS = 8
GDN_V_HEADS = 16
GDN_DIM = 128
CONV_WIDTH = 4
CHUNK = 64
NEG = -1e30
VMEM_LIMIT = 56 * 1024 * 1024

NT_DIMS = (((1,), (1,)), ((), ()))
TN_DIMS = (((0,), (0,)), ((), ()))


def _rms(x, gain):
    return x * lax.rsqrt(jnp.mean(x * x, axis=-1, keepdims=True) + EPS) * gain


def _dot(a, b):
    return jnp.dot(a, b, preferred_element_type=F32)


def _inproj_body(x_ref, gain_ref, w_ref, wba_ref, out_ref, ba_ref, u_scr):
    @pl.when(pl.program_id(1) == 0)
    def _():
        u = _rms(x_ref[...], gain_ref[...]).astype(BF16)
        u_scr[...] = u
        ba_ref[...] = _dot(u, wba_ref[...])

    out_ref[...] = _dot(u_scr[...], w_ref[...]).astype(out_ref.dtype)


def _in_proj(x2d, gain, w_main, w_ba, *, tm=512, tn=1024):
    n, d = x2d.shape
    width = w_main.shape[1]
    return pl.pallas_call(
        _inproj_body,
        grid=(n // tm, width // tn),
        in_specs=[
            pl.BlockSpec((tm, d), lambda i, j: (i, 0)),
            pl.BlockSpec((1, d), lambda i, j: (0, 0)),
            pl.BlockSpec((d, tn), lambda i, j: (0, j)),
            pl.BlockSpec((d, LANES), lambda i, j: (0, 0)),
        ],
        out_specs=[
            pl.BlockSpec((tm, tn), lambda i, j: (i, j)),
            pl.BlockSpec((tm, LANES), lambda i, j: (i, 0)),
        ],
        out_shape=[
            jax.ShapeDtypeStruct((n, width), BF16),
            jax.ShapeDtypeStruct((n, LANES), F32),
        ],
        scratch_shapes=[pltpu.VMEM((tm, d), BF16)],
        compiler_params=pltpu.CompilerParams(
            dimension_semantics=("parallel", "arbitrary"), vmem_limit_bytes=VMEM_LIMIT),
        name="in_proj",
    )(x2d, gain, w_main, w_ba)


def _attn_body(lq1_ref, lk1_ref, lq2_ref, lk2_ref, gain_ref, q_ref, k_ref, v_ref, o_ref,
               *, tq, lambda_init):
    qi = pl.program_id(2)
    q = q_ref[...] * jnp.asarray(DA_HEAD_DIM ** -0.5, BF16)
    lane = lax.broadcasted_iota(jnp.int32, q.shape, 1)
    zero = jnp.zeros_like(q)
    qs = jnp.concatenate([jnp.where(lane < DA_HEAD_DIM, q, zero),
                          jnp.where(lane >= DA_HEAD_DIM, q, zero)], axis=0)

    def block(kb, carry, masked):
        m, l, acc = carry
        start = pl.multiple_of(kb * tq, tq)
        k = k_ref[pl.ds(start, tq), :]
        v = v_ref[pl.ds(start, tq), :]
        s = lax.dot_general(qs, k, NT_DIMS, preferred_element_type=F32)
        if masked:
            row = lax.broadcasted_iota(jnp.int32, s.shape, 0)
            col = lax.broadcasted_iota(jnp.int32, s.shape, 1)
            row = jnp.where(row >= tq, row - tq, row)
            s = jnp.where(col <= row, s, NEG)
        m_new = jnp.maximum(m, jnp.max(s, axis=-1, keepdims=True))
        alpha = jnp.exp(m - m_new)
        p = jnp.exp(s - m_new)
        l = alpha * l + jnp.sum(p, axis=-1, keepdims=True)
        acc = alpha * acc + _dot(p.astype(BF16), v)
        return m_new, l, acc

    init = (jnp.full((2 * tq, 1), NEG, F32), jnp.zeros((2 * tq, 1), F32),
            jnp.zeros((2 * tq, LANES), F32))
    carry = lax.fori_loop(0, qi, lambda kb, c: block(kb, c, False), init)
    _, l, acc = block(qi, carry, True)
    on = acc / l
    lam = (jnp.exp(jnp.sum(lq1_ref[...] * lk1_ref[...], axis=-1, keepdims=True))
           - jnp.exp(jnp.sum(lq2_ref[...] * lk2_ref[...], axis=-1, keepdims=True)) + lambda_init)
    o = on[:tq] - lam * on[tq:]
    o_ref[...] = (_rms(o, gain_ref[...]) * (1.0 - lambda_init)).astype(o_ref.dtype)


def _diff_attn(proj3d, lq1, lk1, lq2, lk2, sub_gain, *, q_col, k_col, v_col, lambda_init, tq=256):
    b, t, _ = proj3d.shape
    vec = pl.BlockSpec((1, DA_HEAD_DIM), lambda bi, h, qi: (0, 0))
    return pl.pallas_call(
        functools.partial(_attn_body, tq=tq, lambda_init=lambda_init),
        grid=(b, DA_HEADS, t // tq),
        in_specs=[
            vec, vec, vec, vec,
            pl.BlockSpec((1, LANES), lambda bi, h, qi: (0, 0)),
            pl.BlockSpec((None, tq, LANES), lambda bi, h, qi: (bi, qi, q_col + h)),
            pl.BlockSpec((None, t, LANES), lambda bi, h, qi: (bi, 0, k_col + h)),
            pl.BlockSpec((None, t, LANES), lambda bi, h, qi: (bi, 0, v_col + h)),
        ],
        out_specs=pl.BlockSpec((None, tq, LANES), lambda bi, h, qi: (bi, qi, h)),
        out_shape=jax.ShapeDtypeStruct((b, t, DA_HEADS * LANES), BF16),
        compiler_params=pltpu.CompilerParams(
            dimension_semantics=("parallel", "parallel", "arbitrary"), vmem_limit_bytes=VMEM_LIMIT),
        name="diff_attn",
    )(lq1, lk1, lq2, lk2, sub_gain, proj3d, proj3d, proj3d)


def _softplus(x):
    return jnp.maximum(x, 0.0) + jnp.log1p(jnp.exp(-jnp.abs(x)))


def _gdn_body(qkv_ref, z_ref, ba_ref, convw_ref, prow_ref, pcol_ref, onorm_ref, o_ref,
              xx_scr, state_scr):
    c = CHUNK
    t = pl.program_id(1)

    @pl.when(t == 0)
    def _():
        xx_scr[0:8, :] = jnp.zeros((8, xx_scr.shape[1]), F32)
        state_scr[...] = jnp.zeros(state_scr.shape, F32)

    @pl.when(t > 0)
    def _():
        xx_scr[0:8, :] = xx_scr[c:c + 8, :]

    xx_scr[8:8 + c, :] = qkv_ref[...].astype(F32)

    def conv_silu(slab):
        sl = slice(slab * LANES, (slab + 1) * LANES)
        y = convw_ref[3:4, sl] * xx_scr[8:8 + c, sl]
        for k in range(CONV_WIDTH - 1):
            y = y + convw_ref[k:k + 1, sl] * xx_scr[5 + k:5 + k + c, sl]
        return y * jax.nn.sigmoid(y)

    def l2n(y):
        return y * lax.rsqrt(jnp.sum(y * y, axis=-1, keepdims=True) + EPS)

    ii = lax.broadcasted_iota(jnp.int32, (c, c), 0)
    jj = lax.broadcasted_iota(jnp.int32, (c, c), 1)
    ba = ba_ref[...]
    beta_col = jax.nn.sigmoid(ba)
    g_col = -jnp.exp(prow_ref[0:1, :]) * _softplus(ba + prow_ref[1:2, :])
    tril = jnp.where(ii >= jj, 1.0, 0.0).astype(F32)
    gc_col = jnp.dot(tril, g_col, precision=lax.Precision.HIGHEST, preferred_element_type=F32)
    eye = jnp.where(lax.broadcasted_iota(jnp.int32, (LANES, LANES), 0)
                    == lax.broadcasted_iota(jnp.int32, (LANES, LANES), 1), 1.0, 0.0).astype(F32)
    ba_row = lax.dot_general(eye, ba, NT_DIMS, precision=lax.Precision.HIGHEST,
                             preferred_element_type=F32)
    g_row = -jnp.exp(pcol_ref[:, 0:1]) * _softplus(ba_row + pcol_ref[:, 1:2])
    triu = jnp.where(ii <= jj, 1.0, 0.0).astype(F32)
    gc_row = jnp.dot(g_row, triu, precision=lax.Precision.HIGHEST, preferred_element_type=F32)

    strict = ii > jj
    causal = ii >= jj
    for hk in range(GDN_QK_HEADS):
        q = l2n(conv_silu(hk)) * (GDN_DIM ** -0.5)
        k = l2n(conv_silu(GDN_QK_HEADS + hk))
        kb = k.astype(BF16)
        kq = jnp.concatenate([kb, q.astype(BF16)], axis=0)
        gram = lax.dot_general(kq, kb, NT_DIMS, preferred_element_type=F32)
        kk, qk = gram[:c], gram[c:]
        for r in range(GDN_V_HEADS // GDN_QK_HEADS):
            h = hk * (GDN_V_HEADS // GDN_QK_HEADS) + r
            v = conv_silu(2 * GDN_QK_HEADS + h)
            gcol = gc_col[:, GDN_V_HEADS + h:GDN_V_HEADS + h + 1]
            grow = gc_row[GDN_V_HEADS + h:GDN_V_HEADS + h + 1, :]
            glast = grow[:, c - 1:c]
            bcol = beta_col[:, h:h + 1]
            dec = jnp.where(causal, jnp.exp(gcol - grow), 0.0)
            a = jnp.where(strict, kk * dec, 0.0) * bcol
            intra = qk * dec
            n = -a
            p = a
            for _ in range(5):
                pb = p.astype(BF16)
                p = _dot(pb, pb)
                n = n + p + _dot(n.astype(BF16), p.astype(BF16))
            egc = jnp.exp(gcol)
            vb = v * bcol
            kbg = k * (bcol * egc)
            rhs = jnp.concatenate([vb, kbg], axis=1)
            uw = rhs + _dot(n.astype(BF16), rhs.astype(BF16))
            u, w = uw[:, :GDN_DIM], uw[:, GDN_DIM:]
            qg = q * egc
            kd = k * jnp.exp(glast - gcol)
            s = state_scr[h]
            wq = jnp.concatenate([w.astype(BF16), qg.astype(BF16)], axis=0)
            ws = _dot(wq, s.astype(BF16))
            v_new = u - ws[:c]
            vnb = v_new.astype(BF16)
            o = ws[c:] + _dot(intra.astype(BF16), vnb)
            state_scr[h] = s * jnp.exp(glast) + lax.dot_general(
                kd.astype(BF16), vnb, TN_DIMS, preferred_element_type=F32)
            sl = slice(h * GDN_DIM, (h + 1) * GDN_DIM)
            z = z_ref[:, sl].astype(F32)
            o_ref[:, sl] = (_rms(o, onorm_ref[...]) * (z * jax.nn.sigmoid(z))).astype(o_ref.dtype)


def _gdn(proj3d, ba3d, conv_w, prow, pcol, onorm, *, qkv_col, z_col):
    b, t, _ = proj3d.shape
    c = CHUNK
    conv_ch = conv_w.shape[1]
    v_width = GDN_V_HEADS * GDN_DIM
    return pl.pallas_call(
        _gdn_body,
        grid=(b, t // c),
        in_specs=[
            pl.BlockSpec((None, c, conv_ch), lambda bi, ti: (bi, ti, qkv_col)),
            pl.BlockSpec((None, c, v_width), lambda bi, ti: (bi, ti, z_col)),
            pl.BlockSpec((None, c, LANES), lambda bi, ti: (bi, ti, 0)),
            pl.BlockSpec((CONV_WIDTH, conv_ch), lambda bi, ti: (0, 0)),
            pl.BlockSpec((8, LANES), lambda bi, ti: (0, 0)),
            pl.BlockSpec((LANES, LANES), lambda bi, ti: (0, 0)),
            pl.BlockSpec((1, GDN_DIM), lambda bi, ti: (0, 0)),
        ],
        out_specs=pl.BlockSpec((None, c, v_width), lambda bi, ti: (bi, ti, 0)),
        out_shape=jax.ShapeDtypeStruct((b, t, v_width), BF16),
        scratch_shapes=[pltpu.VMEM((c + 8, conv_ch), F32),
                        pltpu.VMEM((GDN_V_HEADS, GDN_DIM, GDN_DIM), F32)],
        compiler_params=pltpu.CompilerParams(
            dimension_semantics=("parallel", "arbitrary"), vmem_limit_bytes=VMEM_LIMIT),
        name="gdn",
    )(proj3d, proj3d, ba3d, conv_w, prow, pcol, onorm)


def _merge_body(x_ref, oa_ref, ob_ref, ga_ref, gb_ref, wa_ref, wb_ref, wo_ref, gain_ref, h_ref):
    ya = _dot(oa_ref[...], wa_ref[...])
    yb = _dot(ob_ref[...], wb_ref[...])
    merged = (jax.nn.sigmoid(ga_ref[...].astype(F32)) * ya
              + jax.nn.sigmoid(gb_ref[...].astype(F32)) * yb)
    mix = _dot(merged.astype(BF16), wo_ref[...])
    h_ref[...] = x_ref[...] + _rms(mix, gain_ref[...])


def _merge(x2d, o_a, o_b, proj, w_a, w_b, w_o, gain, *, ga_col, gb_col, tm=512):
    n, d = x2d.shape
    row = lambda i: (i, 0)
    const = lambda i: (0, 0)
    return pl.pallas_call(
        _merge_body,
        grid=(n // tm,),
        in_specs=[
            pl.BlockSpec((tm, d), row),
            pl.BlockSpec((tm, o_a.shape[1]), row),
            pl.BlockSpec((tm, o_b.shape[1]), row),
            pl.BlockSpec((tm, d), lambda i: (i, ga_col)),
            pl.BlockSpec((tm, d), lambda i: (i, gb_col)),
            pl.BlockSpec(w_a.shape, const),
            pl.BlockSpec(w_b.shape, const),
            pl.BlockSpec(w_o.shape, const),
            pl.BlockSpec((1, d), const),
        ],
        out_specs=pl.BlockSpec((tm, d), row),
        out_shape=jax.ShapeDtypeStruct((n, d), F32),
        compiler_params=pltpu.CompilerParams(
            dimension_semantics=("parallel",), vmem_limit_bytes=VMEM_LIMIT),
        name="merge_out",
    )(x2d, o_a, o_b, proj, proj, w_a, w_b, w_o, gain)


def _mlp_body(h_ref, p_ref, g_pre_ref, wup_ref, wdn_ref, g_post_ref, wple_ref, wgate_ref,
              g_ple_ref, out_ref, *, ff_chunk):
    h = h_ref[...]
    u = _rms(h, g_pre_ref[...]).astype(BF16)
    d_ff = wup_ref.shape[1]
    acc = jnp.zeros(h.shape, F32)
    for c0 in range(0, d_ff, ff_chunk):
        hid = jnp.square(jnp.maximum(_dot(u, wup_ref[:, c0:c0 + ff_chunk]), 0.0))
        acc = acc + _dot(hid.astype(BF16), wdn_ref[c0:c0 + ff_chunk, :])
    h = h + _rms(acc, g_post_ref[...])
    e = _dot(p_ref[...].astype(BF16), wple_ref[...]) * jax.nn.sigmoid(
        _dot(h.astype(BF16), wgate_ref[...]))
    out_ref[...] = h + _rms(e, g_ple_ref[...])


def _mlp(h2d, p2d, g_pre, w_up, w_dn, g_post, w_ple, w_gate, g_ple, *, tm=512, ff_chunk=1024):
    n, d = h2d.shape
    row = lambda i: (i, 0)
    const = lambda i: (0, 0)
    return pl.pallas_call(
        functools.partial(_mlp_body, ff_chunk=ff_chunk),
        grid=(n // tm,),
        in_specs=[
            pl.BlockSpec((tm, d), row),
            pl.BlockSpec((tm, p2d.shape[1]), row),
            pl.BlockSpec((1, d), const),
            pl.BlockSpec(w_up.shape, const),
            pl.BlockSpec(w_dn.shape, const),
            pl.BlockSpec((1, d), const),
            pl.BlockSpec(w_ple.shape, const),
            pl.BlockSpec(w_gate.shape, const),
            pl.BlockSpec((1, d), const),
        ],
        out_specs=pl.BlockSpec((tm, d), row),
        out_shape=jax.ShapeDtypeStruct((n, d), F32),
        compiler_params=pltpu.CompilerParams(
            dimension_semantics=("parallel",), vmem_limit_bytes=VMEM_LIMIT),
        name="mlp_ple",
    )(h2d, p2d, g_pre, w_up, w_dn, g_post, w_ple, w_gate, g_ple)


def _layer(h, p_i, i, pre_mix_norm, w_in, conv_w, lambda_q1, lambda_k1, lambda_q2, lambda_k2,
           da_sub_norm, gdn_a_log, gdn_dt_bias, gdn_out_norm, w_branch_a, w_branch_b, w_out,
           post_mix_norm, pre_mlp_norm, w_up, w_down, post_mlp_norm, w_ple, w_ple_gate, ple_norm):
    b, t, d = h.shape
    n = b * t
    da_w = DA_HEADS * 2 * DA_HEAD_DIM
    gqk_w = GDN_QK_HEADS * GDN_DIM
    gv_w = GDN_V_HEADS * GDN_DIM
    conv_ch = 2 * gqk_w + gv_w
    sizes = (da_w, da_w, da_w, conv_ch, gv_w, GDN_V_HEADS, GDN_V_HEADS, d, d)
    offs = [0]
    for s in sizes:
        offs.append(offs[-1] + s)
    piece = lambda j: w_in[:, offs[j]:offs[j + 1]]
    w_main = jnp.concatenate([piece(3), piece(4), piece(0), piece(1), piece(2), piece(7), piece(8)],
                             axis=1).astype(BF16)
    w_ba = jnp.pad(jnp.concatenate([piece(5), piece(6)], axis=1),
                   ((0, 0), (0, LANES - 2 * GDN_V_HEADS))).astype(BF16)
    col = {"conv": 0, "z": conv_ch, "q": conv_ch + gv_w, "k": conv_ch + gv_w + da_w,
           "v": conv_ch + gv_w + 2 * da_w, "ga": conv_ch + gv_w + 3 * da_w,
           "gb": conv_ch + gv_w + 3 * da_w + d}

    x2d = h.reshape(n, d)
    proj, ba = _in_proj(x2d, pre_mix_norm.reshape(1, d), w_main, w_ba)
    proj3d = proj.reshape(b, t, -1)

    lambda_init = 0.8 - 0.6 * math.exp(-0.3 * i)
    o_a = _diff_attn(proj3d, lambda_q1.reshape(1, -1), lambda_k1.reshape(1, -1),
                     lambda_q2.reshape(1, -1), lambda_k2.reshape(1, -1), da_sub_norm.reshape(1, -1),
                     q_col=col["q"] // LANES, k_col=col["k"] // LANES, v_col=col["v"] // LANES,
                     lambda_init=lambda_init)

    lanes = jnp.zeros((LANES,), F32)
    a_log = lanes.at[GDN_V_HEADS:2 * GDN_V_HEADS].set(gdn_a_log.astype(F32))
    dt_b = lanes.at[GDN_V_HEADS:2 * GDN_V_HEADS].set(gdn_dt_bias.astype(F32))
    prow = jnp.zeros((8, LANES), F32).at[0].set(a_log).at[1].set(dt_b)
    pcol = jnp.zeros((LANES, LANES), F32).at[:, 0].set(a_log).at[:, 1].set(dt_b)
    o_b = _gdn(proj3d, ba.reshape(b, t, LANES), conv_w, prow, pcol, gdn_out_norm.reshape(1, -1),
               qkv_col=col["conv"] // conv_ch, z_col=col["z"] // gv_w)

    h1 = _merge(x2d, o_a.reshape(n, -1), o_b.reshape(n, -1), proj, w_branch_a.astype(BF16),
                w_branch_b.astype(BF16), w_out.astype(BF16), post_mix_norm.reshape(1, d),
                ga_col=col["ga"] // d, gb_col=col["gb"] // d)
    out = _mlp(h1, p_i.reshape(n, -1), pre_mlp_norm.reshape(1, d), w_up.astype(BF16),
               w_down.astype(BF16), post_mlp_norm.reshape(1, d), w_ple.astype(BF16),
               w_ple_gate.astype(BF16), ple_norm.reshape(1, d))
    return out.reshape(b, t, d)


def kernel(x, p, pre_mix_norm, w_in, conv_w, lambda_q1, lambda_k1, lambda_q2, lambda_k2,
           da_sub_norm, gdn_a_log, gdn_dt_bias, gdn_out_norm, w_branch_a, w_branch_b, w_out,
           post_mix_norm, pre_mlp_norm, w_up, w_down, post_mlp_norm, w_ple, w_ple_gate, ple_norm):
    per_layer = (pre_mix_norm, w_in, conv_w, lambda_q1, lambda_k1, lambda_q2, lambda_k2,
                 da_sub_norm, gdn_a_log, gdn_dt_bias, gdn_out_norm, w_branch_a, w_branch_b, w_out,
                 post_mix_norm, pre_mlp_norm, w_up, w_down, post_mlp_norm, w_ple, w_ple_gate,
                 ple_norm)
    h = x
    for i in range(p.shape[0]):
        h = _layer(h, p[i], i, *(w[i] for w in per_layer))
    return h
```

```python
import functools
import math

import jax
import jax.numpy as jnp
from jax import lax
from jax.experimental import pallas as pl
from jax.experimental.pallas import tpu as pltpu

F32 = jnp.float32
BF16 = jnp.bfloat16

EPS = 1e-6
LANES = 128
DA_HEADS = 8
DA_HEAD_DIM = 64
GDN_QK_HEADS = 8
GDN_V_HEADS = 16
GDN_DIM = 128
CONV_WIDTH = 4
CHUNK = 64
GDN_GROUP = 8
NEG = -1e30
VMEM_LIMIT = 56 * 1024 * 1024

NT_DIMS = (((1,), (1,)), ((), ()))
TN_DIMS = (((0,), (0,)), ((), ()))


def _rms(x, gain):
    return x * lax.rsqrt(jnp.mean(x * x, axis=-1, keepdims=True) + EPS) * gain


def _dot(a, b):
    return jnp.dot(a, b, preferred_element_type=F32)


def _inproj_body(x_ref, gain_ref, w_ref, wba_ref, out_ref, ba_ref, u_scr):
    @pl.when(pl.program_id(1) == 0)
    def _():
        u = _rms(x_ref[...], gain_ref[...]).astype(BF16)
        u_scr[...] = u
        ba_ref[...] = _dot(u, wba_ref[...])

    out_ref[...] = _dot(u_scr[...], w_ref[...]).astype(out_ref.dtype)


def _in_proj(x2d, gain, w_main, w_ba, *, tm=512, tn=1024):
    n, d = x2d.shape
    width = w_main.shape[1]
    return pl.pallas_call(
        _inproj_body,
        grid=(n // tm, width // tn),
        in_specs=[
            pl.BlockSpec((tm, d), lambda i, j: (i, 0)),
            pl.BlockSpec((1, d), lambda i, j: (0, 0)),
            pl.BlockSpec((d, tn), lambda i, j: (0, j)),
            pl.BlockSpec((d, LANES), lambda i, j: (0, 0)),
        ],
        out_specs=[
            pl.BlockSpec((tm, tn), lambda i, j: (i, j)),
            pl.BlockSpec((tm, LANES), lambda i, j: (i, 0)),
        ],
        out_shape=[
            jax.ShapeDtypeStruct((n, width), BF16),
            jax.ShapeDtypeStruct((n, LANES), F32),
        ],
        scratch_shapes=[pltpu.VMEM((tm, d), BF16)],
        compiler_params=pltpu.CompilerParams(
            dimension_semantics=("parallel", "arbitrary"), vmem_limit_bytes=VMEM_LIMIT),
        name="in_proj",
    )(x2d, gain, w_main, w_ba)


def _attn_body(lq1_ref, lk1_ref, lq2_ref, lk2_ref, gain_ref, q_ref, k_ref, v_ref, o_ref,
               *, tq, lambda_init):
    qi = pl.program_id(2)
    q = q_ref[...] * jnp.asarray(DA_HEAD_DIM ** -0.5, BF16)
    lane = lax.broadcasted_iota(jnp.int32, q.shape, 1)
    zero = jnp.zeros_like(q)
    qs = jnp.concatenate([jnp.where(lane < DA_HEAD_DIM, q, zero),
                          jnp.where(lane >= DA_HEAD_DIM, q, zero)], axis=0)

    def block(kb, carry, masked):
        m, l, acc = carry
        start = pl.multiple_of(kb * tq, tq)
        k = k_ref[pl.ds(start, tq), :]
        v = v_ref[pl.ds(start, tq), :]
        s = lax.dot_general(qs, k, NT_DIMS, preferred_element_type=F32)
        if masked:
            row = lax.broadcasted_iota(jnp.int32, s.shape, 0)
            col = lax.broadcasted_iota(jnp.int32, s.shape, 1)
            row = jnp.where(row >= tq, row - tq, row)
            s = jnp.where(col <= row, s, NEG)
        m_new = jnp.maximum(m, jnp.max(s, axis=-1, keepdims=True))
        alpha = jnp.exp(m - m_new)
        p = jnp.exp(s - m_new)
        l = alpha * l + jnp.sum(p, axis=-1, keepdims=True)
        acc = alpha * acc + _dot(p.astype(BF16), v)
        return m_new, l, acc

    init = (jnp.full((2 * tq, 1), NEG, F32), jnp.zeros((2 * tq, 1), F32),
            jnp.zeros((2 * tq, LANES), F32))
    carry = lax.fori_loop(0, qi, lambda kb, c: block(kb, c, False), init)
    _, l, acc = block(qi, carry, True)
    on = acc / l
    lam = (jnp.exp(jnp.sum(lq1_ref[...] * lk1_ref[...], axis=-1, keepdims=True))
           - jnp.exp(jnp.sum(lq2_ref[...] * lk2_ref[...], axis=-1, keepdims=True)) + lambda_init)
    o = on[:tq] - lam * on[tq:]
    o_ref[...] = (_rms(o, gain_ref[...]) * (1.0 - lambda_init)).astype(o_ref.dtype)


def _diff_attn(proj3d, lq1, lk1, lq2, lk2, sub_gain, *, q_col, k_col, v_col, lambda_init, tq=256):
    b, t, _ = proj3d.shape
    vec = pl.BlockSpec((1, DA_HEAD_DIM), lambda bi, h, qi: (0, 0))
    return pl.pallas_call(
        functools.partial(_attn_body, tq=tq, lambda_init=lambda_init),
        grid=(b, DA_HEADS, t // tq),
        in_specs=[
            vec, vec, vec, vec,
            pl.BlockSpec((1, LANES), lambda bi, h, qi: (0, 0)),
            pl.BlockSpec((None, tq, LANES), lambda bi, h, qi: (bi, qi, q_col + h)),
            pl.BlockSpec((None, t, LANES), lambda bi, h, qi: (bi, 0, k_col + h)),
            pl.BlockSpec((None, t, LANES), lambda bi, h, qi: (bi, 0, v_col + h)),
        ],
        out_specs=pl.BlockSpec((None, tq, LANES), lambda bi, h, qi: (bi, qi, h)),
        out_shape=jax.ShapeDtypeStruct((b, t, DA_HEADS * LANES), BF16),
        compiler_params=pltpu.CompilerParams(
            dimension_semantics=("parallel", "parallel", "arbitrary"), vmem_limit_bytes=VMEM_LIMIT),
        name="diff_attn",
    )(lq1, lk1, lq2, lk2, sub_gain, proj3d, proj3d, proj3d)


def _softplus(x):
    return jnp.maximum(x, 0.0) + jnp.log1p(jnp.exp(-jnp.abs(x)))


def _gdn_body(qkv_ref, z_ref, ba_ref, convw_ref, prow_ref, pcol_ref, onorm_ref, o_ref,
              xx_scr, state_scr):
    c = CHUNK
    t = pl.program_id(1)

    @pl.when(t == 0)
    def _():
        xx_scr[0:8, :] = jnp.zeros((8, xx_scr.shape[1]), F32)
        state_scr[...] = jnp.zeros(state_scr.shape, F32)

    @pl.when(t > 0)
    def _():
        xx_scr[0:8, :] = xx_scr[c:c + 8, :]

    xx_scr[8:8 + c, :] = qkv_ref[...].astype(F32)

    def conv_silu(slab):
        sl = slice(slab * LANES, (slab + 1) * LANES)
        y = convw_ref[3:4, sl] * xx_scr[8:8 + c, sl]
        for k in range(CONV_WIDTH - 1):
            y = y + convw_ref[k:k + 1, sl] * xx_scr[5 + k:5 + k + c, sl]
        return y * jax.nn.sigmoid(y)

    def l2n(y):
        return y * lax.rsqrt(jnp.sum(y * y, axis=-1, keepdims=True) + EPS)

    ii = lax.broadcasted_iota(jnp.int32, (c, c), 0)
    jj = lax.broadcasted_iota(jnp.int32, (c, c), 1)
    ba = ba_ref[...]
    beta_col = jax.nn.sigmoid(ba)
    g_col = -jnp.exp(prow_ref[0:1, :]) * _softplus(ba + prow_ref[1:2, :])
    tril = jnp.where(ii >= jj, 1.0, 0.0).astype(F32)
    gc_col = jnp.dot(tril, g_col, precision=lax.Precision.HIGHEST, preferred_element_type=F32)
    eye = jnp.where(lax.broadcasted_iota(jnp.int32, (LANES, LANES), 0)
                    == lax.broadcasted_iota(jnp.int32, (LANES, LANES), 1), 1.0, 0.0).astype(F32)
    ba_row = lax.dot_general(eye, ba, NT_DIMS, precision=lax.Precision.HIGHEST,
                             preferred_element_type=F32)
    g_row = -jnp.exp(pcol_ref[:, 0:1]) * _softplus(ba_row + pcol_ref[:, 1:2])
    triu = jnp.where(ii <= jj, 1.0, 0.0).astype(F32)
    gc_row = jnp.dot(g_row, triu, precision=lax.Precision.HIGHEST, preferred_element_type=F32)

    strict = ii > jj
    causal = ii >= jj
    rep = GDN_V_HEADS // GDN_QK_HEADS
    bf = lambda y: y.astype(BF16)
    for g0 in range(0, GDN_QK_HEADS, GDN_GROUP):
        hks = list(range(g0, g0 + GDN_GROUP))
        hs = [hk * rep + r for hk in hks for r in range(rep)]
        q = {hk: l2n(conv_silu(hk)) * (GDN_DIM ** -0.5) for hk in hks}
        k = {hk: l2n(conv_silu(GDN_QK_HEADS + hk)) for hk in hks}
        gram = {hk: lax.dot_general(jnp.concatenate([bf(k[hk]), bf(q[hk])], axis=0), bf(k[hk]),
                                    NT_DIMS, preferred_element_type=F32) for hk in hks}
        gcol = {h: gc_col[:, GDN_V_HEADS + h:GDN_V_HEADS + h + 1] for h in hs}
        grow = {h: gc_row[GDN_V_HEADS + h:GDN_V_HEADS + h + 1, :] for h in hs}
        glast = {h: grow[h][:, c - 1:c] for h in hs}
        bcol = {h: beta_col[:, h:h + 1] for h in hs}
        dec = {h: jnp.where(causal, jnp.exp(gcol[h] - grow[h]), 0.0) for h in hs}
        a = {h: jnp.where(strict, gram[h // rep][:c] * dec[h], 0.0) * bcol[h] for h in hs}
        intra = {h: gram[h // rep][c:] * dec[h] for h in hs}
        n = {h: -a[h] for h in hs}
        p = {h: _dot(bf(a[h]), bf(a[h])) for h in hs}
        for _ in range(4):
            r_ = {h: _dot(jnp.concatenate([bf(n[h]), bf(p[h])], axis=0), bf(p[h])) for h in hs}
            n = {h: n[h] + p[h] + r_[h][:c] for h in hs}
            p = {h: r_[h][c:] for h in hs}
        n = {h: n[h] + p[h] + _dot(bf(n[h]), bf(p[h])) for h in hs}
        egc = {h: jnp.exp(gcol[h]) for h in hs}
        rhs = {h: jnp.concatenate([conv_silu(2 * GDN_QK_HEADS + h) * bcol[h],
                                   k[h // rep] * (bcol[h] * egc[h])], axis=1) for h in hs}
        uw = {h: rhs[h] + _dot(bf(n[h]), bf(rhs[h])) for h in hs}
        s = {h: state_scr[h] for h in hs}
        ws = {h: _dot(jnp.concatenate([bf(uw[h][:, GDN_DIM:]), bf(q[h // rep] * egc[h])], axis=0),
                      bf(s[h])) for h in hs}
        vnb = {h: bf(uw[h][:, :GDN_DIM] - ws[h][:c]) for h in hs}
        o = {h: ws[h][c:] + _dot(bf(intra[h]), vnb[h]) for h in hs}
        for h in hs:
            kd = k[h // rep] * jnp.exp(glast[h] - gcol[h])
            state_scr[h] = s[h] * jnp.exp(glast[h]) + lax.dot_general(
                bf(kd), vnb[h], TN_DIMS, preferred_element_type=F32)
        for h in hs:
            sl = slice(h * GDN_DIM, (h + 1) * GDN_DIM)
            z = z_ref[:, sl].astype(F32)
            o_ref[:, sl] = (_rms(o[h], onorm_ref[...]) * (z * jax.nn.sigmoid(z))).astype(o_ref.dtype)


def _gdn(proj3d, ba3d, conv_w, prow, pcol, onorm, *, qkv_col, z_col):
    b, t, _ = proj3d.shape
    c = CHUNK
    conv_ch = conv_w.shape[1]
    v_width = GDN_V_HEADS * GDN_DIM
    return pl.pallas_call(
        _gdn_body,
        grid=(b, t // c),
        in_specs=[
            pl.BlockSpec((None, c, conv_ch), lambda bi, ti: (bi, ti, qkv_col)),
            pl.BlockSpec((None, c, v_width), lambda bi, ti: (bi, ti, z_col)),
            pl.BlockSpec((None, c, LANES), lambda bi, ti: (bi, ti, 0)),
            pl.BlockSpec((CONV_WIDTH, conv_ch), lambda bi, ti: (0, 0)),
            pl.BlockSpec((8, LANES), lambda bi, ti: (0, 0)),
            pl.BlockSpec((LANES, LANES), lambda bi, ti: (0, 0)),
            pl.BlockSpec((1, GDN_DIM), lambda bi, ti: (0, 0)),
        ],
        out_specs=pl.BlockSpec((None, c, v_width), lambda bi, ti: (bi, ti, 0)),
        out_shape=jax.ShapeDtypeStruct((b, t, v_width), BF16),
        scratch_shapes=[pltpu.VMEM((c + 8, conv_ch), F32),
                        pltpu.VMEM((GDN_V_HEADS, GDN_DIM, GDN_DIM), F32)],
        compiler_params=pltpu.CompilerParams(
            dimension_semantics=("parallel", "arbitrary"), vmem_limit_bytes=VMEM_LIMIT),
        name="gdn",
    )(proj3d, proj3d, ba3d, conv_w, prow, pcol, onorm)


def _merge_body(x_ref, oa_ref, ob_ref, ga_ref, gb_ref, wa_ref, wb_ref, wo_ref, gain_ref, h_ref):
    ya = _dot(oa_ref[...], wa_ref[...])
    yb = _dot(ob_ref[...], wb_ref[...])
    merged = (jax.nn.sigmoid(ga_ref[...].astype(F32)) * ya
              + jax.nn.sigmoid(gb_ref[...].astype(F32)) * yb)
    mix = _dot(merged.astype(BF16), wo_ref[...])
    h_ref[...] = x_ref[...] + _rms(mix, gain_ref[...])


def _merge(x2d, o_a, o_b, proj, w_a, w_b, w_o, gain, *, ga_col, gb_col, tm=512):
    n, d = x2d.shape
    row = lambda i: (i, 0)
    const = lambda i: (0, 0)
    return pl.pallas_call(
        _merge_body,
        grid=(n // tm,),
        in_specs=[
            pl.BlockSpec((tm, d), row),
            pl.BlockSpec((tm, o_a.shape[1]), row),
            pl.BlockSpec((tm, o_b.shape[1]), row),
            pl.BlockSpec((tm, d), lambda i: (i, ga_col)),
            pl.BlockSpec((tm, d), lambda i: (i, gb_col)),
            pl.BlockSpec(w_a.shape, const),
            pl.BlockSpec(w_b.shape, const),
            pl.BlockSpec(w_o.shape, const),
            pl.BlockSpec((1, d), const),
        ],
        out_specs=pl.BlockSpec((tm, d), row),
        out_shape=jax.ShapeDtypeStruct((n, d), F32),
        compiler_params=pltpu.CompilerParams(
            dimension_semantics=("parallel",), vmem_limit_bytes=VMEM_LIMIT),
        name="merge_out",
    )(x2d, o_a, o_b, proj, proj, w_a, w_b, w_o, gain)


def _mlp_body(h_ref, p_ref, g_pre_ref, wup_ref, wdn_ref, g_post_ref, wple_ref, wgate_ref,
              g_ple_ref, out_ref, *, ff_chunk):
    h = h_ref[...]
    u = _rms(h, g_pre_ref[...]).astype(BF16)
    d_ff = wup_ref.shape[1]
    acc = jnp.zeros(h.shape, F32)
    for c0 in range(0, d_ff, ff_chunk):
        hid = jnp.square(jnp.maximum(_dot(u, wup_ref[:, c0:c0 + ff_chunk]), 0.0))
        acc = acc + _dot(hid.astype(BF16), wdn_ref[c0:c0 + ff_chunk, :])
    h = h + _rms(acc, g_post_ref[...])
    e = _dot(p_ref[...].astype(BF16), wple_ref[...]) * jax.nn.sigmoid(
        _dot(h.astype(BF16), wgate_ref[...]))
    out_ref[...] = h + _rms(e, g_ple_ref[...])


def _mlp(h2d, p2d, g_pre, w_up, w_dn, g_post, w_ple, w_gate, g_ple, *, tm=512, ff_chunk=1024):
    n, d = h2d.shape
    row = lambda i: (i, 0)
    const = lambda i: (0, 0)
    return pl.pallas_call(
        functools.partial(_mlp_body, ff_chunk=ff_chunk),
        grid=(n // tm,),
        in_specs=[
            pl.BlockSpec((tm, d), row),
            pl.BlockSpec((tm, p2d.shape[1]), row),
            pl.BlockSpec((1, d), const),
            pl.BlockSpec(w_up.shape, const),
            pl.BlockSpec(w_dn.shape, const),
            pl.BlockSpec((1, d), const),
            pl.BlockSpec(w_ple.shape, const),
            pl.BlockSpec(w_gate.shape, const),
            pl.BlockSpec((1, d), const),
        ],
        out_specs=pl.BlockSpec((tm, d), row),
        out_shape=jax.ShapeDtypeStruct((n, d), F32),
        compiler_params=pltpu.CompilerParams(
            dimension_semantics=("parallel",), vmem_limit_bytes=VMEM_LIMIT),
        name="mlp_ple",
    )(h2d, p2d, g_pre, w_up, w_dn, g_post, w_ple, w_gate, g_ple)


def _layer(h, p_i, i, pre_mix_norm, w_in, conv_w, lambda_q1, lambda_k1, lambda_q2, lambda_k2,
           da_sub_norm, gdn_a_log, gdn_dt_bias, gdn_out_norm, w_branch_a, w_branch_b, w_out,
           post_mix_norm, pre_mlp_norm, w_up, w_down, post_mlp_norm, w_ple, w_ple_gate, ple_norm):
    b, t, d = h.shape
    n = b * t
    da_w = DA_HEADS * 2 * DA_HEAD_DIM
    gqk_w = GDN_QK_HEADS * GDN_DIM
    gv_w = GDN_V_HEADS * GDN_DIM
    conv_ch = 2 * gqk_w + gv_w
    sizes = (da_w, da_w, da_w, conv_ch, gv_w, GDN_V_HEADS, GDN_V_HEADS, d, d)
    offs = [0]
    for s in sizes:
        offs.append(offs[-1] + s)
    piece = lambda j: w_in[:, offs[j]:offs[j + 1]]
    w_main = jnp.concatenate([piece(3), piece(4), piece(0), piece(1), piece(2), piece(7), piece(8)],
                             axis=1).astype(BF16)
    w_ba = jnp.pad(jnp.concatenate([piece(5), piece(6)], axis=1),
                   ((0, 0), (0, LANES - 2 * GDN_V_HEADS))).astype(BF16)
    col = {"conv": 0, "z": conv_ch, "q": conv_ch + gv_w, "k": conv_ch + gv_w + da_w,
           "v": conv_ch + gv_w + 2 * da_w, "ga": conv_ch + gv_w + 3 * da_w,
           "gb": conv_ch + gv_w + 3 * da_w + d}

    x2d = h.reshape(n, d)
    proj, ba = _in_proj(x2d, pre_mix_norm.reshape(1, d), w_main, w_ba)
    proj3d = proj.reshape(b, t, -1)

    lambda_init = 0.8 - 0.6 * math.exp(-0.3 * i)
    o_a = _diff_attn(proj3d, lambda_q1.reshape(1, -1), lambda_k1.reshape(1, -1),
                     lambda_q2.reshape(1, -1), lambda_k2.reshape(1, -1), da_sub_norm.reshape(1, -1),
                     q_col=col["q"] // LANES, k_col=col["k"] // LANES, v_col=col["v"] // LANES,
                     lambda_init=lambda_init)

    lanes = jnp.zeros((LANES,), F32)
    a_log = lanes.at[GDN_V_HEADS:2 * GDN_V_HEADS].set(gdn_a_log.astype(F32))
    dt_b = lanes.at[GDN_V_HEADS:2 * GDN_V_HEADS].set(gdn_dt_bias.astype(F32))
    prow = jnp.zeros((8, LANES), F32).at[0].set(a_log).at[1].set(dt_b)
    pcol = jnp.zeros((LANES, LANES), F32).at[:, 0].set(a_log).at[:, 1].set(dt_b)
    o_b = _gdn(proj3d, ba.reshape(b, t, LANES), conv_w, prow, pcol, gdn_out_norm.reshape(1, -1),
               qkv_col=col["conv"] // conv_ch, z_col=col["z"] // gv_w)

    h1 = _merge(x2d, o_a.reshape(n, -1), o_b.reshape(n, -1), proj, w_branch_a.astype(BF16),
                w_branch_b.astype(BF16), w_out.astype(BF16), post_mix_norm.reshape(1, d),
                ga_col=col["ga"] // d, gb_col=col["gb"] // d)
    out = _mlp(h1, p_i.reshape(n, -1), pre_mlp_norm.reshape(1, d), w_up.astype(BF16),
               w_down.astype(BF16), post_mlp_norm.reshape(1, d), w_ple.astype(BF16),
               w_ple_gate.astype(BF16), ple_norm.reshape(1, d))
    return out.reshape(b, t, d)


def kernel(x, p, pre_mix_norm, w_in, conv_w, lambda_q1, lambda_k1, lambda_q2, lambda_k2,
           da_sub_norm, gdn_a_log, gdn_dt_bias, gdn_out_norm, w_branch_a, w_branch_b, w_out,
           post_mix_norm, pre_mlp_norm, w_up, w_down, post_mlp_norm, w_ple, w_ple_gate, ple_norm):
    per_layer = (pre_mix_norm, w_in, conv_w, lambda_q1, lambda_k1, lambda_q2, lambda_k2,
                 da_sub_norm, gdn_a_log, gdn_dt_bias, gdn_out_norm, w_branch_a, w_branch_b, w_out,
                 post_mix_norm, pre_mlp_norm, w_up, w_down, post_mlp_norm, w_ple, w_ple_gate,
                 ple_norm)
    h = x
    for i in range(p.shape[0]):
        h = _layer(h, p[i], i, *(w[i] for w in per_layer))
    return h
```

```python
import functools
import math

import jax
import jax.numpy as jnp
from jax import lax
from jax.experimental import pallas as pl
from jax.experimental.pallas import tpu as pltpu

F32 = jnp.float32
BF16 = jnp.bfloat16

EPS = 1e-6
LANES = 128
DA_HEADS = 8
DA_HEAD_DIM = 64
GDN_QK_HEADS = 8
GDN_V_HEADS = 16
GDN_DIM = 128
CONV_WIDTH = 4
CHUNK = 64
ATTN_CHAINS = 2
GDN_GROUP = 8
NEG = -1e30
VMEM_LIMIT = 56 * 1024 * 1024

NT_DIMS = (((1,), (1,)), ((), ()))
TN_DIMS = (((0,), (0,)), ((), ()))


def _rms(x, gain):
    return x * lax.rsqrt(jnp.mean(x * x, axis=-1, keepdims=True) + EPS) * gain


def _dot(a, b):
    return jnp.dot(a, b, preferred_element_type=F32)


def _inproj_body(x_ref, gain_ref, w_ref, wba_ref, out_ref, ba_ref, u_scr):
    @pl.when(pl.program_id(1) == 0)
    def _():
        u = _rms(x_ref[...], gain_ref[...]).astype(BF16)
        u_scr[...] = u
        ba_ref[...] = _dot(u, wba_ref[...])

    out_ref[...] = _dot(u_scr[...], w_ref[...]).astype(out_ref.dtype)


def _in_proj(x2d, gain, w_main, w_ba, *, tm=1024, tn=1024):
    n, d = x2d.shape
    width = w_main.shape[1]
    return pl.pallas_call(
        _inproj_body,
        grid=(n // tm, width // tn),
        in_specs=[
            pl.BlockSpec((tm, d), lambda i, j: (i, 0)),
            pl.BlockSpec((1, d), lambda i, j: (0, 0)),
            pl.BlockSpec((d, tn), lambda i, j: (0, j)),
            pl.BlockSpec((d, LANES), lambda i, j: (0, 0)),
        ],
        out_specs=[
            pl.BlockSpec((tm, tn), lambda i, j: (i, j)),
            pl.BlockSpec((tm, LANES), lambda i, j: (i, 0)),
        ],
        out_shape=[
            jax.ShapeDtypeStruct((n, width), BF16),
            jax.ShapeDtypeStruct((n, LANES), F32),
        ],
        scratch_shapes=[pltpu.VMEM((tm, d), BF16)],
        compiler_params=pltpu.CompilerParams(
            dimension_semantics=("parallel", "arbitrary"), vmem_limit_bytes=VMEM_LIMIT),
        name="in_proj",
    )(x2d, gain, w_main, w_ba)


def _interleave(chains):
    chains = list(chains)
    while chains:
        for g in list(chains):
            if next(g, StopIteration) is StopIteration:
                chains.remove(g)


def _attn_body(lq1_ref, lk1_ref, lq2_ref, lk2_ref, gain_ref, q_ref, k_ref, v_ref, o_ref,
               qt_scr, vt_scr, *, tq, lambda_init):
    t = q_ref.shape[0]
    nq = t // tq
    scale = jnp.asarray(DA_HEAD_DIM ** -0.5, BF16)
    for i in range(nq):
        sl = slice(i * tq, (i + 1) * tq)
        qt_scr[:, sl] = (q_ref[sl, :] * scale).T
        vt_scr[:, sl] = v_ref[sl, :].T
    lam = (jnp.exp(jnp.sum(lq1_ref[...] * lk1_ref[...], axis=-1, keepdims=True))
           - jnp.exp(jnp.sum(lq2_ref[...] * lk2_ref[...], axis=-1, keepdims=True)) + lambda_init)
    feat = lax.broadcasted_iota(jnp.int32, (LANES, tq), 0)
    kv_row = lax.broadcasted_iota(jnp.int32, (tq, 2 * tq), 0)
    q_col = lax.broadcasted_iota(jnp.int32, (tq, 2 * tq), 1)
    on_or_below_diag = kv_row <= jnp.where(q_col >= tq, q_col - tq, q_col)

    def chain(tiles):
        for qi in tiles:
            qsl = slice(qi * tq, (qi + 1) * tq)
            qt = qt_scr[:, qsl]
            zero = jnp.zeros_like(qt)
            qs = jnp.concatenate([jnp.where(feat < DA_HEAD_DIM, qt, zero),
                                  jnp.where(feat >= DA_HEAD_DIM, qt, zero)], axis=1)
            scores = lambda j: _dot(k_ref[j * tq:(j + 1) * tq, :], qs)
            m = jnp.full((1, 2 * tq), NEG, F32)
            l = jnp.zeros((1, 2 * tq), F32)
            acc = jnp.zeros((LANES, 2 * tq), F32)
            s_next = scores(0)
            yield
            for j in range(qi + 1):
                s = s_next
                if j < qi:
                    s_next = scores(j + 1)
                    yield
                else:
                    s = jnp.where(on_or_below_diag, s, NEG)
                m_new = jnp.maximum(m, jnp.max(s, axis=0, keepdims=True))
                alpha = jnp.exp(m - m_new)
                p = jnp.exp(s - m_new)
                l = alpha * l + jnp.sum(p, axis=0, keepdims=True)
                pv = _dot(vt_scr[:, j * tq:(j + 1) * tq], p.astype(BF16))
                yield
                acc = alpha * acc + pv
                m = m_new
            on = acc * (1.0 / l)
            o = on[:, :tq] - lam * on[:, tq:]
            y = o * lax.rsqrt(jnp.mean(o * o, axis=0, keepdims=True) + EPS)
            o_ref[qsl, :] = (y.T * gain_ref[...] * (1.0 - lambda_init)).astype(o_ref.dtype)

    pairs = [[nq - 1 - i, i] for i in range(nq // 2)]
    for c0 in range(0, len(pairs), ATTN_CHAINS):
        _interleave(chain(tiles) for tiles in pairs[c0:c0 + ATTN_CHAINS])


def _diff_attn(proj3d, lq1, lk1, lq2, lk2, sub_gain, *, q_col, k_col, v_col, lambda_init, tq=256):
    b, t, _ = proj3d.shape
    vec = pl.BlockSpec((1, DA_HEAD_DIM), lambda bi, h: (0, 0))
    head = lambda col: pl.BlockSpec((None, t, LANES), lambda bi, h: (bi, 0, col + h))
    return pl.pallas_call(
        functools.partial(_attn_body, tq=tq, lambda_init=lambda_init),
        grid=(b, DA_HEADS),
        in_specs=[vec, vec, vec, vec, pl.BlockSpec((1, LANES), lambda bi, h: (0, 0)),
                  head(q_col), head(k_col), head(v_col)],
        out_specs=pl.BlockSpec((None, t, LANES), lambda bi, h: (bi, 0, h)),
        out_shape=jax.ShapeDtypeStruct((b, t, DA_HEADS * LANES), BF16),
        scratch_shapes=[pltpu.VMEM((LANES, t), BF16), pltpu.VMEM((LANES, t), BF16)],
        compiler_params=pltpu.CompilerParams(
            dimension_semantics=("parallel", "parallel"), vmem_limit_bytes=VMEM_LIMIT),
        name="diff_attn",
    )(lq1, lk1, lq2, lk2, sub_gain, proj3d, proj3d, proj3d)


def _softplus(x):
    return jnp.maximum(x, 0.0) + jnp.log1p(jnp.exp(-jnp.abs(x)))


def _gdn_body(qkv_ref, z_ref, ba_ref, convw_ref, prow_ref, pcol_ref, onorm_ref, o_ref,
              xx_scr, state_scr):
    c = CHUNK
    t = pl.program_id(1)

    @pl.when(t == 0)
    def _():
        xx_scr[0:8, :] = jnp.zeros((8, xx_scr.shape[1]), F32)
        state_scr[...] = jnp.zeros(state_scr.shape, F32)

    @pl.when(t > 0)
    def _():
        xx_scr[0:8, :] = xx_scr[c:c + 8, :]

    xx_scr[8:8 + c, :] = qkv_ref[...].astype(F32)

    def conv_silu(slab):
        sl = slice(slab * LANES, (slab + 1) * LANES)
        y = convw_ref[3:4, sl] * xx_scr[8:8 + c, sl]
        for k in range(CONV_WIDTH - 1):
            y = y + convw_ref[k:k + 1, sl] * xx_scr[5 + k:5 + k + c, sl]
        return y * jax.nn.sigmoid(y)

    def l2n(y):
        return y * lax.rsqrt(jnp.sum(y * y, axis=-1, keepdims=True) + EPS)

    ii = lax.broadcasted_iota(jnp.int32, (c, c), 0)
    jj = lax.broadcasted_iota(jnp.int32, (c, c), 1)
    ba = ba_ref[...]
    beta_col = jax.nn.sigmoid(ba)
    g_col = -jnp.exp(prow_ref[0:1, :]) * _softplus(ba + prow_ref[1:2, :])
    tril = jnp.where(ii >= jj, 1.0, 0.0).astype(F32)
    gc_col = jnp.dot(tril, g_col, precision=lax.Precision.HIGHEST, preferred_element_type=F32)
    eye = jnp.where(lax.broadcasted_iota(jnp.int32, (LANES, LANES), 0)
                    == lax.broadcasted_iota(jnp.int32, (LANES, LANES), 1), 1.0, 0.0).astype(F32)
    ba_row = lax.dot_general(eye, ba, NT_DIMS, precision=lax.Precision.HIGHEST,
                             preferred_element_type=F32)
    g_row = -jnp.exp(pcol_ref[:, 0:1]) * _softplus(ba_row + pcol_ref[:, 1:2])
    triu = jnp.where(ii <= jj, 1.0, 0.0).astype(F32)
    gc_row = jnp.dot(g_row, triu, precision=lax.Precision.HIGHEST, preferred_element_type=F32)

    strict = ii > jj
    causal = ii >= jj
    rep = GDN_V_HEADS // GDN_QK_HEADS
    bf = lambda y: y.astype(BF16)
    for g0 in range(0, GDN_QK_HEADS, GDN_GROUP):
        hks = list(range(g0, g0 + GDN_GROUP))
        hs = [hk * rep + r for hk in hks for r in range(rep)]
        q = {hk: l2n(conv_silu(hk)) * (GDN_DIM ** -0.5) for hk in hks}
        k = {hk: l2n(conv_silu(GDN_QK_HEADS + hk)) for hk in hks}
        gram = {hk: lax.dot_general(jnp.concatenate([bf(k[hk]), bf(q[hk])], axis=0), bf(k[hk]),
                                    NT_DIMS, preferred_element_type=F32) for hk in hks}
        gcol = {h: gc_col[:, GDN_V_HEADS + h:GDN_V_HEADS + h + 1] for h in hs}
        grow = {h: gc_row[GDN_V_HEADS + h:GDN_V_HEADS + h + 1, :] for h in hs}
        glast = {h: grow[h][:, c - 1:c] for h in hs}
        bcol = {h: beta_col[:, h:h + 1] for h in hs}
        dec = {h: jnp.where(causal, jnp.exp(gcol[h] - grow[h]), 0.0) for h in hs}
        a = {h: jnp.where(strict, gram[h // rep][:c] * dec[h], 0.0) * bcol[h] for h in hs}
        intra = {h: gram[h // rep][c:] * dec[h] for h in hs}
        n = {h: -a[h] for h in hs}
        p = {h: _dot(bf(a[h]), bf(a[h])) for h in hs}
        for _ in range(4):
            r_ = {h: _dot(jnp.concatenate([bf(n[h]), bf(p[h])], axis=0), bf(p[h])) for h in hs}
            n = {h: n[h] + p[h] + r_[h][:c] for h in hs}
            p = {h: r_[h][c:] for h in hs}
        n = {h: n[h] + p[h] + _dot(bf(n[h]), bf(p[h])) for h in hs}
        egc = {h: jnp.exp(gcol[h]) for h in hs}
        rhs = {h: jnp.concatenate([conv_silu(2 * GDN_QK_HEADS + h) * bcol[h],
                                   k[h // rep] * (bcol[h] * egc[h])], axis=1) for h in hs}
        uw = {h: rhs[h] + _dot(bf(n[h]), bf(rhs[h])) for h in hs}
        s = {h: state_scr[h] for h in hs}
        ws = {h: _dot(jnp.concatenate([bf(uw[h][:, GDN_DIM:]), bf(q[h // rep] * egc[h])], axis=0),
                      bf(s[h])) for h in hs}
        vnb = {h: bf(uw[h][:, :GDN_DIM] - ws[h][:c]) for h in hs}
        o = {h: ws[h][c:] + _dot(bf(intra[h]), vnb[h]) for h in hs}
        for h in hs:
            kd = k[h // rep] * jnp.exp(glast[h] - gcol[h])
            state_scr[h] = s[h] * jnp.exp(glast[h]) + lax.dot_general(
                bf(kd), vnb[h], TN_DIMS, preferred_element_type=F32)
        for h in hs:
            sl = slice(h * GDN_DIM, (h + 1) * GDN_DIM)
            z = z_ref[:, sl].astype(F32)
            o_ref[:, sl] = (_rms(o[h], onorm_ref[...]) * (z * jax.nn.sigmoid(z))).astype(o_ref.dtype)


def _gdn(proj3d, ba3d, conv_w, prow, pcol, onorm, *, qkv_col, z_col):
    b, t, _ = proj3d.shape
    c = CHUNK
    conv_ch = conv_w.shape[1]
    v_width = GDN_V_HEADS * GDN_DIM
    return pl.pallas_call(
        _gdn_body,
        grid=(b, t // c),
        in_specs=[
            pl.BlockSpec((None, c, conv_ch), lambda bi, ti: (bi, ti, qkv_col)),
            pl.BlockSpec((None, c, v_width), lambda bi, ti: (bi, ti, z_col)),
            pl.BlockSpec((None, c, LANES), lambda bi, ti: (bi, ti, 0)),
            pl.BlockSpec((CONV_WIDTH, conv_ch), lambda bi, ti: (0, 0)),
            pl.BlockSpec((8, LANES), lambda bi, ti: (0, 0)),
            pl.BlockSpec((LANES, LANES), lambda bi, ti: (0, 0)),
            pl.BlockSpec((1, GDN_DIM), lambda bi, ti: (0, 0)),
        ],
        out_specs=pl.BlockSpec((None, c, v_width), lambda bi, ti: (bi, ti, 0)),
        out_shape=jax.ShapeDtypeStruct((b, t, v_width), BF16),
        scratch_shapes=[pltpu.VMEM((c + 8, conv_ch), F32),
                        pltpu.VMEM((GDN_V_HEADS, GDN_DIM, GDN_DIM), F32)],
        compiler_params=pltpu.CompilerParams(
            dimension_semantics=("parallel", "arbitrary"), vmem_limit_bytes=VMEM_LIMIT),
        name="gdn",
    )(proj3d, proj3d, ba3d, conv_w, prow, pcol, onorm)


def _merge_body(x_ref, oa_ref, ob_ref, ga_ref, gb_ref, wa_ref, wb_ref, wo_ref, gain_ref, h_ref):
    ya = _dot(oa_ref[...], wa_ref[...])
    yb = _dot(ob_ref[...], wb_ref[...])
    merged = (jax.nn.sigmoid(ga_ref[...].astype(F32)) * ya
              + jax.nn.sigmoid(gb_ref[...].astype(F32)) * yb)
    mix = _dot(merged.astype(BF16), wo_ref[...])
    h_ref[...] = x_ref[...] + _rms(mix, gain_ref[...])


def _merge(x2d, o_a, o_b, proj, w_a, w_b, w_o, gain, *, ga_col, gb_col, tm=512):
    n, d = x2d.shape
    row = lambda i: (i, 0)
    const = lambda i: (0, 0)
    return pl.pallas_call(
        _merge_body,
        grid=(n // tm,),
        in_specs=[
            pl.BlockSpec((tm, d), row),
            pl.BlockSpec((tm, o_a.shape[1]), row),
            pl.BlockSpec((tm, o_b.shape[1]), row),
            pl.BlockSpec((tm, d), lambda i: (i, ga_col)),
            pl.BlockSpec((tm, d), lambda i: (i, gb_col)),
            pl.BlockSpec(w_a.shape, const),
            pl.BlockSpec(w_b.shape, const),
            pl.BlockSpec(w_o.shape, const),
            pl.BlockSpec((1, d), const),
        ],
        out_specs=pl.BlockSpec((tm, d), row),
        out_shape=jax.ShapeDtypeStruct((n, d), F32),
        compiler_params=pltpu.CompilerParams(
            dimension_semantics=("parallel",), vmem_limit_bytes=VMEM_LIMIT),
        name="merge_out",
    )(x2d, o_a, o_b, proj, proj, w_a, w_b, w_o, gain)


def _mlp_body(h_ref, p_ref, g_pre_ref, wup_ref, wdn_ref, g_post_ref, wple_ref, wgate_ref,
              g_ple_ref, out_ref, *, ff_chunk):
    h = h_ref[...]
    u = _rms(h, g_pre_ref[...]).astype(BF16)
    d_ff = wup_ref.shape[1]
    acc = jnp.zeros(h.shape, F32)
    for c0 in range(0, d_ff, ff_chunk):
        hid = jnp.square(jnp.maximum(_dot(u, wup_ref[:, c0:c0 + ff_chunk]), 0.0))
        acc = acc + _dot(hid.astype(BF16), wdn_ref[c0:c0 + ff_chunk, :])
    h = h + _rms(acc, g_post_ref[...])
    e = _dot(p_ref[...].astype(BF16), wple_ref[...]) * jax.nn.sigmoid(
        _dot(h.astype(BF16), wgate_ref[...]))
    out_ref[...] = h + _rms(e, g_ple_ref[...])


def _mlp(h2d, p2d, g_pre, w_up, w_dn, g_post, w_ple, w_gate, g_ple, *, tm=512, ff_chunk=1024):
    n, d = h2d.shape
    row = lambda i: (i, 0)
    const = lambda i: (0, 0)
    return pl.pallas_call(
        functools.partial(_mlp_body, ff_chunk=ff_chunk),
        grid=(n // tm,),
        in_specs=[
            pl.BlockSpec((tm, d), row),
            pl.BlockSpec((tm, p2d.shape[1]), row),
            pl.BlockSpec((1, d), const),
            pl.BlockSpec(w_up.shape, const),
            pl.BlockSpec(w_dn.shape, const),
            pl.BlockSpec((1, d), const),
            pl.BlockSpec(w_ple.shape, const),
            pl.BlockSpec(w_gate.shape, const),
            pl.BlockSpec((1, d), const),
        ],
        out_specs=pl.BlockSpec((tm, d), row),
        out_shape=jax.ShapeDtypeStruct((n, d), F32),
        compiler_params=pltpu.CompilerParams(
            dimension_semantics=("parallel",), vmem_limit_bytes=VMEM_LIMIT),
        name="mlp_ple",
    )(h2d, p2d, g_pre, w_up, w_dn, g_post, w_ple, w_gate, g_ple)


def _layer(h, p_i, i, pre_mix_norm, w_in, conv_w, lambda_q1, lambda_k1, lambda_q2, lambda_k2,
           da_sub_norm, gdn_a_log, gdn_dt_bias, gdn_out_norm, w_branch_a, w_branch_b, w_out,
           post_mix_norm, pre_mlp_norm, w_up, w_down, post_mlp_norm, w_ple, w_ple_gate, ple_norm):
    b, t, d = h.shape
    n = b * t
    da_w = DA_HEADS * 2 * DA_HEAD_DIM
    gqk_w = GDN_QK_HEADS * GDN_DIM
    gv_w = GDN_V_HEADS * GDN_DIM
    conv_ch = 2 * gqk_w + gv_w
    sizes = (da_w, da_w, da_w, conv_ch, gv_w, GDN_V_HEADS, GDN_V_HEADS, d, d)
    offs = [0]
    for s in sizes:
        offs.append(offs[-1] + s)
    piece = lambda j: w_in[:, offs[j]:offs[j + 1]]
    w_main = jnp.concatenate([piece(3), piece(4), piece(0), piece(1), piece(2), piece(7), piece(8)],
                             axis=1).astype(BF16)
    w_ba = jnp.pad(jnp.concatenate([piece(5), piece(6)], axis=1),
                   ((0, 0), (0, LANES - 2 * GDN_V_HEADS))).astype(BF16)
    col = {"conv": 0, "z": conv_ch, "q": conv_ch + gv_w, "k": conv_ch + gv_w + da_w,
           "v": conv_ch + gv_w + 2 * da_w, "ga": conv_ch + gv_w + 3 * da_w,
           "gb": conv_ch + gv_w + 3 * da_w + d}

    x2d = h.reshape(n, d)
    proj, ba = _in_proj(x2d, pre_mix_norm.reshape(1, d), w_main, w_ba)
    proj3d = proj.reshape(b, t, -1)

    lambda_init = 0.8 - 0.6 * math.exp(-0.3 * i)
    o_a = _diff_attn(proj3d, lambda_q1.reshape(1, -1), lambda_k1.reshape(1, -1),
                     lambda_q2.reshape(1, -1), lambda_k2.reshape(1, -1), da_sub_norm.reshape(1, -1),
                     q_col=col["q"] // LANES, k_col=col["k"] // LANES, v_col=col["v"] // LANES,
                     lambda_init=lambda_init)

    lanes = jnp.zeros((LANES,), F32)
    a_log = lanes.at[GDN_V_HEADS:2 * GDN_V_HEADS].set(gdn_a_log.astype(F32))
    dt_b = lanes.at[GDN_V_HEADS:2 * GDN_V_HEADS].set(gdn_dt_bias.astype(F32))
    prow = jnp.zeros((8, LANES), F32).at[0].set(a_log).at[1].set(dt_b)
    pcol = jnp.zeros((LANES, LANES), F32).at[:, 0].set(a_log).at[:, 1].set(dt_b)
    o_b = _gdn(proj3d, ba.reshape(b, t, LANES), conv_w, prow, pcol, gdn_out_norm.reshape(1, -1),
               qkv_col=col["conv"] // conv_ch, z_col=col["z"] // gv_w)

    h1 = _merge(x2d, o_a.reshape(n, -1), o_b.reshape(n, -1), proj, w_branch_a.astype(BF16),
                w_branch_b.astype(BF16), w_out.astype(BF16), post_mix_norm.reshape(1, d),
                ga_col=col["ga"] // d, gb_col=col["gb"] // d)
    out = _mlp(h1, p_i.reshape(n, -1), pre_mlp_norm.reshape(1, d), w_up.astype(BF16),
               w_down.astype(BF16), post_mlp_norm.reshape(1, d), w_ple.astype(BF16),
               w_ple_gate.astype(BF16), ple_norm.reshape(1, d))
    return out.reshape(b, t, d)


def kernel(x, p, pre_mix_norm, w_in, conv_w, lambda_q1, lambda_k1, lambda_q2, lambda_k2,
           da_sub_norm, gdn_a_log, gdn_dt_bias, gdn_out_norm, w_branch_a, w_branch_b, w_out,
           post_mix_norm, pre_mlp_norm, w_up, w_down, post_mlp_norm, w_ple, w_ple_gate, ple_norm):
    per_layer = (pre_mix_norm, w_in, conv_w, lambda_q1, lambda_k1, lambda_q2, lambda_k2,
                 da_sub_norm, gdn_a_log, gdn_dt_bias, gdn_out_norm, w_branch_a, w_branch_b, w_out,
                 post_mix_norm, pre_mlp_norm, w_up, w_down, post_mlp_norm, w_ple, w_ple_gate,
                 ple_norm)
    h = x
    for i in range(p.shape[0]):
        h = _layer(h, p[i], i, *(w[i] for w in per_layer))
    return h
```

```python
import functools
import math

import jax
import jax.numpy as jnp
from jax import lax
from jax.experimental import pallas as pl
from jax.experimental.pallas import tpu as pltpu

F32 = jnp.float32
BF16 = jnp.bfloat16

EPS = 1e-6
LANES = 128
DA_HEADS = 8
DA_HEAD_DIM = 64
GDN_QK_HEADS = 8
GDN_V_HEADS = 16
GDN_DIM = 128
CONV_WIDTH = 4
CHUNK = 64
ATTN_CHAINS = 4
NEG = -1e30
VMEM_LIMIT = 56 * 1024 * 1024

NT_DIMS = (((1,), (1,)), ((), ()))
TN_DIMS = (((0,), (0,)), ((), ()))


def _rms(x, gain):
    return x * lax.rsqrt(jnp.mean(x * x, axis=-1, keepdims=True) + EPS) * gain


def _dot(a, b):
    return jnp.dot(a, b, preferred_element_type=F32)


def _inproj_body(x_ref, gain_ref, w_ref, wba_ref, out_ref, ba_ref, u_scr):
    @pl.when(pl.program_id(1) == 0)
    def _():
        u = _rms(x_ref[...], gain_ref[...]).astype(BF16)
        u_scr[...] = u
        ba_ref[...] = _dot(u, wba_ref[...])

    out_ref[...] = _dot(u_scr[...], w_ref[...]).astype(out_ref.dtype)


def _in_proj(x2d, gain, w_main, w_ba, *, tm=1024, tn=1024):
    n, d = x2d.shape
    width = w_main.shape[1]
    return pl.pallas_call(
        _inproj_body,
        grid=(n // tm, width // tn),
        in_specs=[
            pl.BlockSpec((tm, d), lambda i, j: (i, 0)),
            pl.BlockSpec((1, d), lambda i, j: (0, 0)),
            pl.BlockSpec((d, tn), lambda i, j: (0, j)),
            pl.BlockSpec((d, LANES), lambda i, j: (0, 0)),
        ],
        out_specs=[
            pl.BlockSpec((tm, tn), lambda i, j: (i, j)),
            pl.BlockSpec((tm, LANES), lambda i, j: (i, 0)),
        ],
        out_shape=[
            jax.ShapeDtypeStruct((n, width), BF16),
            jax.ShapeDtypeStruct((n, LANES), F32),
        ],
        scratch_shapes=[pltpu.VMEM((tm, d), BF16)],
        compiler_params=pltpu.CompilerParams(
            dimension_semantics=("parallel", "arbitrary"), vmem_limit_bytes=VMEM_LIMIT),
        name="in_proj",
    )(x2d, gain, w_main, w_ba)


def _interleave(chains):
    chains = list(chains)
    while chains:
        for g in list(chains):
            if next(g, StopIteration) is StopIteration:
                chains.remove(g)


def _attn_body(lq1_ref, lk1_ref, lq2_ref, lk2_ref, gain_ref, q_ref, k_ref, v_ref, o_ref,
               qt_scr, vt_scr, *, tq, lambda_init):
    t = q_ref.shape[0]
    nq = t // tq
    scale = jnp.asarray(DA_HEAD_DIM ** -0.5, BF16)
    for i in range(nq):
        sl = slice(i * tq, (i + 1) * tq)
        qt_scr[:, sl] = (q_ref[sl, :] * scale).T
        vt_scr[0:LANES, sl] = v_ref[sl, :].T
    vt_scr[LANES:, :] = jnp.ones((vt_scr.shape[0] - LANES, t), BF16)
    lam = (jnp.exp(jnp.sum(lq1_ref[...] * lk1_ref[...], axis=-1, keepdims=True))
           - jnp.exp(jnp.sum(lq2_ref[...] * lk2_ref[...], axis=-1, keepdims=True)) + lambda_init)
    feat = lax.broadcasted_iota(jnp.int32, (LANES, tq), 0)
    kv_row = lax.broadcasted_iota(jnp.int32, (tq, 2 * tq), 0)
    q_col = lax.broadcasted_iota(jnp.int32, (tq, 2 * tq), 1)
    on_or_below_diag = kv_row <= jnp.where(q_col >= tq, q_col - tq, q_col)

    def chain(tiles):
        for qi in tiles:
            qsl = slice(qi * tq, (qi + 1) * tq)
            qt = qt_scr[:, qsl]
            zero = jnp.zeros_like(qt)
            qs = jnp.concatenate([jnp.where(feat < DA_HEAD_DIM, qt, zero),
                                  jnp.where(feat >= DA_HEAD_DIM, qt, zero)], axis=1)
            scores = lambda j: _dot(k_ref[j * tq:(j + 1) * tq, :], qs)
            m = jnp.full((1, 2 * tq), NEG, F32)
            acc = jnp.zeros((vt_scr.shape[0], 2 * tq), F32)
            s_next = scores(0)
            yield
            for j in range(qi + 1):
                s = s_next
                if j < qi:
                    s_next = scores(j + 1)
                    yield
                else:
                    s = jnp.where(on_or_below_diag, s, NEG)
                m_new = jnp.maximum(m, jnp.max(s, axis=0, keepdims=True))
                alpha = jnp.exp2(m - m_new)
                p = jnp.exp2(s - m_new)
                pv = _dot(vt_scr[:, j * tq:(j + 1) * tq], p.astype(BF16))
                yield
                acc = alpha * acc + pv
                m = m_new
            on = acc[0:LANES] * (1.0 / acc[LANES:LANES + 1])
            o = on[:, :tq] - lam * on[:, tq:]
            y = o * lax.rsqrt(jnp.mean(o * o, axis=0, keepdims=True) + EPS)
            o_ref[qsl, :] = (y.T * gain_ref[...] * (1.0 - lambda_init)).astype(o_ref.dtype)

    pairs = [[nq - 1 - i, i] for i in range(nq // 2)]
    for c0 in range(0, len(pairs), ATTN_CHAINS):
        _interleave(chain(tiles) for tiles in pairs[c0:c0 + ATTN_CHAINS])


def _diff_attn(proj3d, lq1, lk1, lq2, lk2, sub_gain, *, q_col, k_col, v_col, lambda_init, tq=256):
    b, t, _ = proj3d.shape
    vec = pl.BlockSpec((1, DA_HEAD_DIM), lambda bi, h: (0, 0))
    head = lambda col: pl.BlockSpec((None, t, LANES), lambda bi, h: (bi, 0, col + h))
    return pl.pallas_call(
        functools.partial(_attn_body, tq=tq, lambda_init=lambda_init),
        grid=(b, DA_HEADS),
        in_specs=[vec, vec, vec, vec, pl.BlockSpec((1, LANES), lambda bi, h: (0, 0)),
                  head(q_col), head(k_col), head(v_col)],
        out_specs=pl.BlockSpec((None, t, LANES), lambda bi, h: (bi, 0, h)),
        out_shape=jax.ShapeDtypeStruct((b, t, DA_HEADS * LANES), BF16),
        scratch_shapes=[pltpu.VMEM((LANES, t), BF16), pltpu.VMEM((LANES + 16, t), BF16)],
        compiler_params=pltpu.CompilerParams(
            dimension_semantics=("parallel", "parallel"), vmem_limit_bytes=VMEM_LIMIT),
        name="diff_attn",
    )(lq1, lk1, lq2, lk2, sub_gain, proj3d, proj3d, proj3d)


def _softplus(x):
    return jnp.maximum(x, 0.0) + jnp.log1p(jnp.exp(-jnp.abs(x)))


def _gdn_body(qkv_ref, z_ref, ba_ref, convw_ref, prow_ref, onorm_ref, o_ref,
              xx_scr, state_scr, u_scr, wq_scr, kd_scr, intra_scr, eg_scr):
    c = CHUNK
    step = pl.program_id(1)
    wslot = step % 2
    rslot = 1 - wslot
    rep = GDN_V_HEADS // GDN_QK_HEADS
    hks = list(range(GDN_QK_HEADS))
    hs = list(range(GDN_V_HEADS))
    bf = lambda y: y.astype(BF16)

    @pl.when(step == 0)
    def _():
        xx_scr[0:8, :] = jnp.zeros((8, xx_scr.shape[1]), F32)
        state_scr[...] = jnp.zeros(state_scr.shape, F32)
        u_scr[1] = jnp.zeros(u_scr.shape[1:], u_scr.dtype)
        wq_scr[1] = jnp.zeros(wq_scr.shape[1:], wq_scr.dtype)
        kd_scr[1] = jnp.zeros(kd_scr.shape[1:], kd_scr.dtype)
        intra_scr[1] = jnp.zeros(intra_scr.shape[1:], intra_scr.dtype)
        eg_scr[1] = jnp.zeros(eg_scr.shape[1:], eg_scr.dtype)

    @pl.when(step > 0)
    def _():
        xx_scr[0:8, :] = xx_scr[c:c + 8, :]

    xx_scr[8:8 + c, :] = qkv_ref[...].astype(F32)

    def conv_silu(slab):
        sl = slice(slab * LANES, (slab + 1) * LANES)
        y = convw_ref[3:4, sl] * xx_scr[8:8 + c, sl]
        for k in range(CONV_WIDTH - 1):
            y = y + convw_ref[k:k + 1, sl] * xx_scr[5 + k:5 + k + c, sl]
        return y * jax.nn.sigmoid(y)

    def l2n(y):
        return y * lax.rsqrt(jnp.sum(y * y, axis=-1, keepdims=True) + EPS)

    def recurrence():
        s = {h: state_scr[h] for h in hs}
        ws = {h: _dot(wq_scr[rslot, h], bf(s[h])) for h in hs}
        yield
        vnb = {h: bf(u_scr[rslot, h] - ws[h][:c]) for h in hs}
        o = {h: ws[h][c:] + _dot(intra_scr[rslot, h], vnb[h]) for h in hs}
        for h in hs:
            state_scr[h] = s[h] * eg_scr[rslot, h][0:1, :] + lax.dot_general(
                kd_scr[rslot, h], vnb[h], TN_DIMS, preferred_element_type=F32)
        yield
        for h in hs:
            sl = slice(h * GDN_DIM, (h + 1) * GDN_DIM)
            z = z_ref[:, sl].astype(F32)
            o_ref[:, sl] = (_rms(o[h], onorm_ref[...]) * (z * jax.nn.sigmoid(z))).astype(o_ref.dtype)

    def prep():
        ii = lax.broadcasted_iota(jnp.int32, (c, c), 0)
        jj = lax.broadcasted_iota(jnp.int32, (c, c), 1)
        strict = ii > jj
        causal = ii >= jj
        hp = lax.Precision.HIGHEST
        ba = ba_ref[...]
        beta_col = jax.nn.sigmoid(ba)
        g_col = -jnp.exp(prow_ref[0:1, :]) * _softplus(ba + prow_ref[1:2, :])
        gc_col = jnp.dot(jnp.where(causal, 1.0, 0.0).astype(F32), g_col, precision=hp,
                         preferred_element_type=F32)
        gc_row = lax.dot_general(g_col, jnp.where(ii <= jj, 1.0, 0.0).astype(F32), TN_DIMS,
                                 precision=hp, preferred_element_type=F32)
        beta_row = lax.dot_general(beta_col, jnp.where(ii == jj, 1.0, 0.0).astype(F32), TN_DIMS,
                                   precision=hp, preferred_element_type=F32)
        g_last = gc_col[c - 1:c, :]
        egc_all = jnp.exp(gc_col)
        kdf_all = pltpu.roll(beta_col, GDN_V_HEADS, axis=1) * jnp.exp(g_last - gc_col)
        eg_all = jnp.exp(g_last)
        gcol, egc, kdf = {}, {}, {}
        for h in hs:
            ln = slice(GDN_V_HEADS + h, GDN_V_HEADS + h + 1)
            gcol[h] = jnp.broadcast_to(gc_col[:, ln], (c, c))
            egc[h] = jnp.broadcast_to(egc_all[:, ln], (c, GDN_DIM))
            kdf[h] = jnp.broadcast_to(kdf_all[:, ln], (c, GDN_DIM))
            eg_scr[wslot, h] = jnp.broadcast_to(eg_all[:, ln], eg_scr.shape[2:])
        q = {hk: l2n(conv_silu(hk)) * (GDN_DIM ** -0.5) for hk in hks}
        k = {hk: l2n(conv_silu(GDN_QK_HEADS + hk)) for hk in hks}
        gram = {hk: lax.dot_general(jnp.concatenate([bf(k[hk]), bf(q[hk])], axis=0), bf(k[hk]),
                                    NT_DIMS, preferred_element_type=F32) for hk in hks}
        yield
        grow = {h: gc_row[GDN_V_HEADS + h:GDN_V_HEADS + h + 1, :] for h in hs}
        brow = {h: beta_row[h:h + 1, :] for h in hs}
        dec = {h: jnp.where(causal, jnp.exp(gcol[h] - grow[h]), 0.0) for h in hs}
        a = {h: jnp.where(strict, gram[h // rep][:c] * dec[h], 0.0) * brow[h] for h in hs}
        for h in hs:
            intra_scr[wslot, h] = bf(gram[h // rep][c:] * dec[h] * brow[h])
        n = {h: -a[h] for h in hs}
        p = {h: _dot(bf(a[h]), bf(a[h])) for h in hs}
        yield
        for _ in range(4):
            r_ = {h: _dot(jnp.concatenate([bf(n[h]), bf(p[h])], axis=0), bf(p[h])) for h in hs}
            yield
            n = {h: n[h] + p[h] + r_[h][:c] for h in hs}
            p = {h: r_[h][c:] for h in hs}
        n = {h: n[h] + p[h] + _dot(bf(n[h]), bf(p[h])) for h in hs}
        yield
        rhs = {h: jnp.concatenate([conv_silu(2 * GDN_QK_HEADS + h), k[h // rep] * egc[h]], axis=1)
               for h in hs}
        uw = {h: rhs[h] + _dot(bf(n[h]), bf(rhs[h])) for h in hs}
        for h in hs:
            u_scr[wslot, h] = uw[h][:, :GDN_DIM]
            wq_scr[wslot, h] = jnp.concatenate([bf(uw[h][:, GDN_DIM:]), bf(q[h // rep] * egc[h])],
                                               axis=0)
            kd_scr[wslot, h] = bf(k[h // rep] * kdf[h])

    _interleave([recurrence(), prep()])


def _gdn(proj3d, ba3d, conv_w, prow, onorm, *, qkv_col, z_col):
    b, t, _ = proj3d.shape
    c = CHUNK
    nt = t // c
    conv_ch = conv_w.shape[1]
    v_width = GDN_V_HEADS * GDN_DIM
    cur = lambda bi, si: (bi, jnp.minimum(si, nt - 1))
    prev = lambda bi, si: (bi, jnp.maximum(si - 1, 0))
    const = lambda bi, si: (0, 0)
    heads = (2, GDN_V_HEADS)
    return pl.pallas_call(
        _gdn_body,
        grid=(b, nt + 1),
        in_specs=[
            pl.BlockSpec((None, c, conv_ch), lambda bi, si: cur(bi, si) + (qkv_col,)),
            pl.BlockSpec((None, c, v_width), lambda bi, si: prev(bi, si) + (z_col,)),
            pl.BlockSpec((None, c, LANES), lambda bi, si: cur(bi, si) + (0,)),
            pl.BlockSpec((CONV_WIDTH, conv_ch), const),
            pl.BlockSpec((8, LANES), const),
            pl.BlockSpec((1, GDN_DIM), const),
        ],
        out_specs=pl.BlockSpec((None, c, v_width), lambda bi, si: prev(bi, si) + (0,)),
        out_shape=jax.ShapeDtypeStruct((b, t, v_width), BF16),
        scratch_shapes=[pltpu.VMEM((c + 8, conv_ch), F32),
                        pltpu.VMEM((GDN_V_HEADS, GDN_DIM, GDN_DIM), F32),
                        pltpu.VMEM(heads + (c, GDN_DIM), F32),
                        pltpu.VMEM(heads + (2 * c, GDN_DIM), BF16),
                        pltpu.VMEM(heads + (c, GDN_DIM), BF16),
                        pltpu.VMEM(heads + (c, c), BF16),
                        pltpu.VMEM(heads + (8, LANES), F32)],
        compiler_params=pltpu.CompilerParams(
            dimension_semantics=("parallel", "arbitrary"), vmem_limit_bytes=VMEM_LIMIT),
        name="gdn",
    )(proj3d, proj3d, ba3d, conv_w, prow, onorm)


def _merge_body(x_ref, oa_ref, ob_ref, ga_ref, gb_ref, wa_ref, wb_ref, wo_ref, gain_ref, h_ref):
    ya = _dot(oa_ref[...], wa_ref[...])
    yb = _dot(ob_ref[...], wb_ref[...])
    merged = (jax.nn.sigmoid(ga_ref[...].astype(F32)) * ya
              + jax.nn.sigmoid(gb_ref[...].astype(F32)) * yb)
    mix = _dot(merged.astype(BF16), wo_ref[...])
    h_ref[...] = x_ref[...] + _rms(mix, gain_ref[...])


def _merge(x2d, o_a, o_b, proj, w_a, w_b, w_o, gain, *, ga_col, gb_col, tm=512):
    n, d = x2d.shape
    row = lambda i: (i, 0)
    const = lambda i: (0, 0)
    return pl.pallas_call(
        _merge_body,
        grid=(n // tm,),
        in_specs=[
            pl.BlockSpec((tm, d), row),
            pl.BlockSpec((tm, o_a.shape[1]), row),
            pl.BlockSpec((tm, o_b.shape[1]), row),
            pl.BlockSpec((tm, d), lambda i: (i, ga_col)),
            pl.BlockSpec((tm, d), lambda i: (i, gb_col)),
            pl.BlockSpec(w_a.shape, const),
            pl.BlockSpec(w_b.shape, const),
            pl.BlockSpec(w_o.shape, const),
            pl.BlockSpec((1, d), const),
        ],
        out_specs=pl.BlockSpec((tm, d), row),
        out_shape=jax.ShapeDtypeStruct((n, d), F32),
        compiler_params=pltpu.CompilerParams(
            dimension_semantics=("parallel",), vmem_limit_bytes=VMEM_LIMIT),
        name="merge_out",
    )(x2d, o_a, o_b, proj, proj, w_a, w_b, w_o, gain)


def _mlp_body(h_ref, p_ref, g_pre_ref, wup_ref, wdn_ref, g_post_ref, wple_ref, wgate_ref,
              g_ple_ref, out_ref, *, ff_chunk):
    h = h_ref[...]
    u = _rms(h, g_pre_ref[...]).astype(BF16)
    d_ff = wup_ref.shape[1]
    acc = jnp.zeros(h.shape, F32)
    for c0 in range(0, d_ff, ff_chunk):
        hid = jnp.square(jnp.maximum(_dot(u, wup_ref[:, c0:c0 + ff_chunk]), 0.0))
        acc = acc + _dot(hid.astype(BF16), wdn_ref[c0:c0 + ff_chunk, :])
    h = h + _rms(acc, g_post_ref[...])
    e = _dot(p_ref[...].astype(BF16), wple_ref[...]) * jax.nn.sigmoid(
        _dot(h.astype(BF16), wgate_ref[...]))
    out_ref[...] = h + _rms(e, g_ple_ref[...])


def _mlp(h2d, p2d, g_pre, w_up, w_dn, g_post, w_ple, w_gate, g_ple, *, tm=512, ff_chunk=1024):
    n, d = h2d.shape
    row = lambda i: (i, 0)
    const = lambda i: (0, 0)
    return pl.pallas_call(
        functools.partial(_mlp_body, ff_chunk=ff_chunk),
        grid=(n // tm,),
        in_specs=[
            pl.BlockSpec((tm, d), row),
            pl.BlockSpec((tm, p2d.shape[1]), row),
            pl.BlockSpec((1, d), const),
            pl.BlockSpec(w_up.shape, const),
            pl.BlockSpec(w_dn.shape, const),
            pl.BlockSpec((1, d), const),
            pl.BlockSpec(w_ple.shape, const),
            pl.BlockSpec(w_gate.shape, const),
            pl.BlockSpec((1, d), const),
        ],
        out_specs=pl.BlockSpec((tm, d), row),
        out_shape=jax.ShapeDtypeStruct((n, d), F32),
        compiler_params=pltpu.CompilerParams(
            dimension_semantics=("parallel",), vmem_limit_bytes=VMEM_LIMIT),
        name="mlp_ple",
    )(h2d, p2d, g_pre, w_up, w_dn, g_post, w_ple, w_gate, g_ple)


def _layer(h, p_i, i, pre_mix_norm, w_in, conv_w, lambda_q1, lambda_k1, lambda_q2, lambda_k2,
           da_sub_norm, gdn_a_log, gdn_dt_bias, gdn_out_norm, w_branch_a, w_branch_b, w_out,
           post_mix_norm, pre_mlp_norm, w_up, w_down, post_mlp_norm, w_ple, w_ple_gate, ple_norm):
    b, t, d = h.shape
    n = b * t
    da_w = DA_HEADS * 2 * DA_HEAD_DIM
    gqk_w = GDN_QK_HEADS * GDN_DIM
    gv_w = GDN_V_HEADS * GDN_DIM
    conv_ch = 2 * gqk_w + gv_w
    sizes = (da_w, da_w, da_w, conv_ch, gv_w, GDN_V_HEADS, GDN_V_HEADS, d, d)
    offs = [0]
    for s in sizes:
        offs.append(offs[-1] + s)
    piece = lambda j: w_in[:, offs[j]:offs[j + 1]]
    w_main = jnp.concatenate([piece(3), piece(4), piece(0) * math.log2(math.e), piece(1), piece(2),
                              piece(7), piece(8)],
                             axis=1).astype(BF16)
    w_ba = jnp.pad(jnp.concatenate([piece(5), piece(6)], axis=1),
                   ((0, 0), (0, LANES - 2 * GDN_V_HEADS))).astype(BF16)
    col = {"conv": 0, "z": conv_ch, "q": conv_ch + gv_w, "k": conv_ch + gv_w + da_w,
           "v": conv_ch + gv_w + 2 * da_w, "ga": conv_ch + gv_w + 3 * da_w,
           "gb": conv_ch + gv_w + 3 * da_w + d}

    x2d = h.reshape(n, d)
    proj, ba = _in_proj(x2d, pre_mix_norm.reshape(1, d), w_main, w_ba)
    proj3d = proj.reshape(b, t, -1)

    lambda_init = 0.8 - 0.6 * math.exp(-0.3 * i)
    o_a = _diff_attn(proj3d, lambda_q1.reshape(1, -1), lambda_k1.reshape(1, -1),
                     lambda_q2.reshape(1, -1), lambda_k2.reshape(1, -1), da_sub_norm.reshape(1, -1),
                     q_col=col["q"] // LANES, k_col=col["k"] // LANES, v_col=col["v"] // LANES,
                     lambda_init=lambda_init)

    lanes = jnp.zeros((LANES,), F32)
    a_log = lanes.at[GDN_V_HEADS:2 * GDN_V_HEADS].set(gdn_a_log.astype(F32))
    dt_b = lanes.at[GDN_V_HEADS:2 * GDN_V_HEADS].set(gdn_dt_bias.astype(F32))
    prow = jnp.zeros((8, LANES), F32).at[0].set(a_log).at[1].set(dt_b)
    o_b = _gdn(proj3d, ba.reshape(b, t, LANES), conv_w, prow, gdn_out_norm.reshape(1, -1),
               qkv_col=col["conv"] // conv_ch, z_col=col["z"] // gv_w)

    h1 = _merge(x2d, o_a.reshape(n, -1), o_b.reshape(n, -1), proj, w_branch_a.astype(BF16),
                w_branch_b.astype(BF16), w_out.astype(BF16), post_mix_norm.reshape(1, d),
                ga_col=col["ga"] // d, gb_col=col["gb"] // d)
    out = _mlp(h1, p_i.reshape(n, -1), pre_mlp_norm.reshape(1, d), w_up.astype(BF16),
               w_down.astype(BF16), post_mlp_norm.reshape(1, d), w_ple.astype(BF16),
               w_ple_gate.astype(BF16), ple_norm.reshape(1, d))
    return out.reshape(b, t, d)


def kernel(x, p, pre_mix_norm, w_in, conv_w, lambda_q1, lambda_k1, lambda_q2, lambda_k2,
           da_sub_norm, gdn_a_log, gdn_dt_bias, gdn_out_norm, w_branch_a, w_branch_b, w_out,
           post_mix_norm, pre_mlp_norm, w_up, w_down, post_mlp_norm, w_ple, w_ple_gate, ple_norm):
    per_layer = (pre_mix_norm, w_in, conv_w, lambda_q1, lambda_k1, lambda_q2, lambda_k2,
                 da_sub_norm, gdn_a_log, gdn_dt_bias, gdn_out_norm, w_branch_a, w_branch_b, w_out,
                 post_mix_norm, pre_mlp_norm, w_up, w_down, post_mlp_norm, w_ple, w_ple_gate,
                 ple_norm)
    h = x
    for i in range(p.shape[0]):
        h = _layer(h, p[i], i, *(w[i] for w in per_layer))
    return h
```

```python
import functools
import math

import jax
import jax.numpy as jnp
from jax import lax
from jax.experimental import pallas as pl
from jax.experimental.pallas import tpu as pltpu

F32 = jnp.float32
BF16 = jnp.bfloat16

EPS = 1e-6
LANES = 128
DA_HEADS = 8
DA_HEAD_DIM = 64
GDN_QK_HEADS = 8
GDN_V_HEADS = 16
GDN_DIM = 128
CONV_WIDTH = 4
CHUNK = 64
ATTN_CHAINS = 4
NEG = -1e30
VMEM_LIMIT = 56 * 1024 * 1024

NT_DIMS = (((1,), (1,)), ((), ()))
TN_DIMS = (((0,), (0,)), ((), ()))


def _rms(x, gain):
    return x * lax.rsqrt(jnp.mean(x * x, axis=-1, keepdims=True) + EPS) * gain


def _dot(a, b):
    return jnp.dot(a, b, preferred_element_type=F32)


def _inproj_body(x_ref, gain_ref, w_ref, wba_ref, out_ref, ba_ref, u_scr):
    @pl.when(pl.program_id(1) == 0)
    def _():
        u = _rms(x_ref[...], gain_ref[...]).astype(BF16)
        u_scr[...] = u
        ba_ref[...] = _dot(u, wba_ref[...])

    out_ref[...] = _dot(u_scr[...], w_ref[...]).astype(out_ref.dtype)


def _in_proj(x2d, gain, w_main, w_ba, *, tm=1024, tn=1024):
    n, d = x2d.shape
    width = w_main.shape[1]
    return pl.pallas_call(
        _inproj_body,
        grid=(n // tm, width // tn),
        in_specs=[
            pl.BlockSpec((tm, d), lambda i, j: (i, 0)),
            pl.BlockSpec((1, d), lambda i, j: (0, 0)),
            pl.BlockSpec((d, tn), lambda i, j: (0, j)),
            pl.BlockSpec((d, LANES), lambda i, j: (0, 0)),
        ],
        out_specs=[
            pl.BlockSpec((tm, tn), lambda i, j: (i, j)),
            pl.BlockSpec((tm, LANES), lambda i, j: (i, 0)),
        ],
        out_shape=[
            jax.ShapeDtypeStruct((n, width), BF16),
            jax.ShapeDtypeStruct((n, LANES), F32),
        ],
        scratch_shapes=[pltpu.VMEM((tm, d), BF16)],
        compiler_params=pltpu.CompilerParams(
            dimension_semantics=("parallel", "arbitrary"), vmem_limit_bytes=VMEM_LIMIT),
        name="in_proj",
    )(x2d, gain, w_main, w_ba)


def _interleave(chains):
    chains = list(chains)
    while chains:
        for g in list(chains):
            if next(g, StopIteration) is StopIteration:
                chains.remove(g)


def _attn_body(lq1_ref, lk1_ref, lq2_ref, lk2_ref, gain_ref, q_ref, k_ref, v_ref, o_ref,
               qt_scr, vt_scr, *, tq, lambda_init):
    t = q_ref.shape[0]
    nq = t // tq
    scale = jnp.asarray(DA_HEAD_DIM ** -0.5, BF16)
    for i in range(nq):
        sl = slice(i * tq, (i + 1) * tq)
        qt_scr[:, sl] = (q_ref[sl, :] * scale).T
        vt_scr[0:LANES, sl] = v_ref[sl, :].T
    vt_scr[LANES:, :] = jnp.ones((vt_scr.shape[0] - LANES, t), BF16)
    lam = (jnp.exp(jnp.sum(lq1_ref[...] * lk1_ref[...], axis=-1, keepdims=True))
           - jnp.exp(jnp.sum(lq2_ref[...] * lk2_ref[...], axis=-1, keepdims=True)) + lambda_init)
    feat = lax.broadcasted_iota(jnp.int32, (LANES, tq), 0)
    kv_row = lax.broadcasted_iota(jnp.int32, (tq, 2 * tq), 0)
    q_col = lax.broadcasted_iota(jnp.int32, (tq, 2 * tq), 1)
    on_or_below_diag = kv_row <= jnp.where(q_col >= tq, q_col - tq, q_col)

    def chain(tiles):
        for qi in tiles:
            qsl = slice(qi * tq, (qi + 1) * tq)
            qt = qt_scr[:, qsl]
            zero = jnp.zeros_like(qt)
            qs = jnp.concatenate([jnp.where(feat < DA_HEAD_DIM, qt, zero),
                                  jnp.where(feat >= DA_HEAD_DIM, qt, zero)], axis=1)
            scores = lambda j: _dot(k_ref[j * tq:(j + 1) * tq, :], qs)
            m = jnp.full((1, 2 * tq), NEG, F32)
            acc = jnp.zeros((vt_scr.shape[0], 2 * tq), F32)
            s_next = scores(0)
            yield
            for j in range(qi + 1):
                s = s_next
                if j < qi:
                    s_next = scores(j + 1)
                    yield
                else:
                    s = jnp.where(on_or_below_diag, s, NEG)
                m_new = jnp.maximum(m, jnp.max(s, axis=0, keepdims=True))
                alpha = jnp.exp2(m - m_new)
                p = jnp.exp2(s - m_new)
                pv = _dot(vt_scr[:, j * tq:(j + 1) * tq], p.astype(BF16))
                yield
                acc = alpha * acc + pv
                m = m_new
            on = acc[0:LANES] * (1.0 / acc[LANES:LANES + 1])
            o = on[:, :tq] - lam * on[:, tq:]
            y = o * lax.rsqrt(jnp.mean(o * o, axis=0, keepdims=True) + EPS)
            o_ref[qsl, :] = (y.T * gain_ref[...] * (1.0 - lambda_init)).astype(o_ref.dtype)

    pairs = [[nq - 1 - i, i] for i in range(nq // 2)]
    for c0 in range(0, len(pairs), ATTN_CHAINS):
        _interleave(chain(tiles) for tiles in pairs[c0:c0 + ATTN_CHAINS])


def _diff_attn(proj3d, lq1, lk1, lq2, lk2, sub_gain, *, q_col, k_col, v_col, lambda_init, tq=256):
    b, t, _ = proj3d.shape
    vec = pl.BlockSpec((1, DA_HEAD_DIM), lambda bi, h: (0, 0))
    head = lambda col: pl.BlockSpec((None, t, LANES), lambda bi, h: (bi, 0, col + h))
    return pl.pallas_call(
        functools.partial(_attn_body, tq=tq, lambda_init=lambda_init),
        grid=(b, DA_HEADS),
        in_specs=[vec, vec, vec, vec, pl.BlockSpec((1, LANES), lambda bi, h: (0, 0)),
                  head(q_col), head(k_col), head(v_col)],
        out_specs=pl.BlockSpec((None, t, LANES), lambda bi, h: (bi, 0, h)),
        out_shape=jax.ShapeDtypeStruct((b, t, DA_HEADS * LANES), BF16),
        scratch_shapes=[pltpu.VMEM((LANES, t), BF16), pltpu.VMEM((LANES + 16, t), BF16)],
        compiler_params=pltpu.CompilerParams(
            dimension_semantics=("parallel", "parallel"), vmem_limit_bytes=VMEM_LIMIT),
        name="diff_attn",
    )(lq1, lk1, lq2, lk2, sub_gain, proj3d, proj3d, proj3d)


def _softplus(x):
    return jnp.maximum(x, 0.0) + jnp.log1p(jnp.exp(-jnp.abs(x)))


def _gdn_body(qkv_ref, z_ref, ba_ref, convw_ref, prow_ref, onorm_ref, o_ref,
              xx_scr, state_scr, u_scr, wq_scr, kd_scr, intra_scr, eg_scr, out_scr):
    c = CHUNK
    step = pl.program_id(1)
    wslot = step % 2
    rslot = 1 - wslot
    rep = GDN_V_HEADS // GDN_QK_HEADS
    hks = list(range(GDN_QK_HEADS))
    hs = list(range(GDN_V_HEADS))
    bf = lambda y: y.astype(BF16)

    @pl.when(step == 0)
    def _():
        xx_scr[:, 0:8, :] = jnp.zeros((xx_scr.shape[0], 8, LANES), F32)
        state_scr[...] = jnp.zeros(state_scr.shape, F32)
        u_scr[1] = jnp.zeros(u_scr.shape[1:], u_scr.dtype)
        wq_scr[1] = jnp.zeros(wq_scr.shape[1:], wq_scr.dtype)
        kd_scr[1] = jnp.zeros(kd_scr.shape[1:], kd_scr.dtype)
        intra_scr[1] = jnp.zeros(intra_scr.shape[1:], intra_scr.dtype)
        eg_scr[1] = jnp.zeros(eg_scr.shape[1:], eg_scr.dtype)

    @pl.when(step > 0)
    def _():
        xx_scr[:, 0:8, :] = xx_scr[:, c:c + 8, :]

    for slab in range(xx_scr.shape[0]):
        xx_scr[slab, 8:8 + c, :] = qkv_ref[:, slab * LANES:(slab + 1) * LANES].astype(F32)

    def row_blocks():
        return [(32 * g + r) for g in range(c // 32) for r in range(4)]

    def perm_time(i):
        return ((i >> 5) << 5) + ((i & 7) << 2) + ((i >> 3) & 3)

    def conv_silu(slab):
        sl = slice(slab * LANES, (slab + 1) * LANES)
        w = [convw_ref[k:k + 1, sl] for k in range(CONV_WIDTH)]
        outs = []
        for g in range(c // 32):
            z = {rho: xx_scr[slab, pl.ds(8 + 32 * g + rho, 8, stride=4), :]
                 for rho in range(1 - CONV_WIDTH, 4)}
            for r in range(4):
                y = w[CONV_WIDTH - 1] * z[r]
                for d in range(1, CONV_WIDTH):
                    y = y + w[CONV_WIDTH - 1 - d] * z[r - d]
                outs.append(y)
        y = jnp.concatenate(outs, axis=0)
        return y * jax.nn.sigmoid(y)

    def l2n(y):
        return y * lax.rsqrt(jnp.sum(y * y, axis=-1, keepdims=True) + EPS)

    lane = lax.broadcasted_iota(jnp.int32, (c, 2 * c), 1)
    first = lane < c
    zeros_c = jnp.zeros((c, 2 * c), BF16)

    def blockdiag(x):
        xb = bf(x)
        return jnp.concatenate([jnp.where(first, xb, zeros_c), jnp.where(first, zeros_c, xb)], axis=0)

    def recurrence():
        s = {h: state_scr[h] for h in hs}
        ws = {h: _dot(wq_scr[rslot, h], bf(s[h])) for h in hs}
        yield
        vnb = {h: bf(u_scr[rslot, h] - ws[h][:c]) for h in hs}
        zv = jnp.zeros((c, GDN_DIM), BF16)
        o = {}
        for hk in hks:
            a_, b_ = rep * hk, rep * hk + 1
            vbd = jnp.concatenate([jnp.concatenate([vnb[a_], zv], axis=1),
                                   jnp.concatenate([zv, vnb[b_]], axis=1)], axis=0)
            ov = _dot(intra_scr[rslot, hk], vbd)
            o[a_] = ws[a_][c:] + ov[:, :GDN_DIM]
            o[b_] = ws[b_][c:] + ov[:, GDN_DIM:]
        for h in hs:
            state_scr[h] = s[h] * eg_scr[rslot, h][0:1, :] + lax.dot_general(
                kd_scr[rslot, h], vnb[h], TN_DIMS, preferred_element_type=F32)
        yield
        for h in hs:
            y = _rms(o[h], onorm_ref[...])
            for m, row0 in enumerate(row_blocks()):
                out_scr[h, pl.ds(row0, 8, stride=4), :] = y[8 * m:8 * m + 8]
        for h in hs:
            sl = slice(h * GDN_DIM, (h + 1) * GDN_DIM)
            z = z_ref[:, sl].astype(F32)
            o_ref[:, sl] = (out_scr[h] * (z * jax.nn.sigmoid(z))).astype(o_ref.dtype)

    def prep():
        ii = perm_time(lax.broadcasted_iota(jnp.int32, (c, c), 0))
        jj = perm_time(lax.broadcasted_iota(jnp.int32, (c, c), 1))
        row2 = perm_time(lax.broadcasted_iota(jnp.int32, (c, 2 * c), 0))
        col2 = perm_time(jnp.where(first, lane, lane - c))
        strict = row2 > col2
        causal = row2 >= col2
        hp = lax.Precision.HIGHEST
        ba = jnp.concatenate([ba_ref[pl.ds(row0, 8, stride=4), :] for row0 in row_blocks()],
                             axis=0)
        beta_col = jax.nn.sigmoid(ba)
        g_col = -jnp.exp(prow_ref[0:1, :]) * _softplus(ba + prow_ref[1:2, :])
        gc_col = jnp.dot(jnp.where(ii >= jj, 1.0, 0.0).astype(F32), g_col, precision=hp,
                         preferred_element_type=F32)
        dup = lambda m: jnp.concatenate([m, m], axis=1)
        gc_row = lax.dot_general(g_col, dup(jnp.where(ii <= jj, 1.0, 0.0).astype(F32)), TN_DIMS,
                                 precision=hp, preferred_element_type=F32)
        beta_row = lax.dot_general(beta_col, dup(jnp.where(ii == jj, 1.0, 0.0).astype(F32)), TN_DIMS,
                                   precision=hp, preferred_element_type=F32)
        g_last = gc_col[c - 1:c, :]
        egc_all = jnp.exp(gc_col)
        kdf_all = pltpu.roll(beta_col, GDN_V_HEADS, axis=1) * jnp.exp(g_last - gc_col)
        eg_all = jnp.exp(g_last)
        gcol, egc, kdf = {}, {}, {}
        for h in hs:
            ln = slice(GDN_V_HEADS + h, GDN_V_HEADS + h + 1)
            gcol[h] = jnp.broadcast_to(gc_col[:, ln], (c, 2 * c))
            egc[h] = jnp.broadcast_to(egc_all[:, ln], (c, GDN_DIM))
            kdf[h] = jnp.broadcast_to(kdf_all[:, ln], (c, GDN_DIM))
            eg_scr[wslot, h] = jnp.broadcast_to(eg_all[:, ln], eg_scr.shape[2:])
        q = {hk: l2n(conv_silu(hk)) * (GDN_DIM ** -0.5) for hk in hks}
        k = {hk: l2n(conv_silu(GDN_QK_HEADS + hk)) for hk in hks}
        gram = {hk: lax.dot_general(jnp.concatenate([bf(k[hk]), bf(q[hk])], axis=0),
                                    jnp.concatenate([bf(k[hk]), bf(k[hk])], axis=0),
                                    NT_DIMS, preferred_element_type=F32) for hk in hks}
        yield
        pick = lambda m, hk: jnp.where(first[0:1, :], m[rep * hk:rep * hk + 1, :],
                                       m[rep * hk + 1:rep * hk + 2, :])
        a = {}
        for hk in hks:
            gcol2 = jnp.where(first, gcol[rep * hk], gcol[rep * hk + 1])
            dec = jnp.where(causal, jnp.exp(gcol2 - pick(gc_row[GDN_V_HEADS:], hk)), 0.0)
            brow = pick(beta_row, hk)
            a[hk] = jnp.where(strict, gram[hk][:c] * dec, 0.0) * brow
            intra_scr[wslot, hk] = bf(gram[hk][c:] * dec * brow)
        n = {hk: -a[hk] for hk in hks}
        p = {hk: _dot(bf(a[hk]), blockdiag(a[hk])) for hk in hks}
        yield
        for _ in range(4):
            r_ = {hk: _dot(jnp.concatenate([bf(n[hk]), bf(p[hk])], axis=0), blockdiag(p[hk]))
                  for hk in hks}
            yield
            n = {hk: n[hk] + p[hk] + r_[hk][:c] for hk in hks}
            p = {hk: r_[hk][c:] for hk in hks}
        n = {hk: n[hk] + p[hk] + _dot(bf(n[hk]), blockdiag(p[hk])) for hk in hks}
        yield
        rhs = {h: jnp.concatenate([conv_silu(2 * GDN_QK_HEADS + h), k[h // rep] * egc[h]], axis=1)
               for h in hs}
        zr = jnp.zeros((c, 2 * GDN_DIM), BF16)
        for hk in hks:
            a_, b_ = rep * hk, rep * hk + 1
            rbd = jnp.concatenate([jnp.concatenate([bf(rhs[a_]), zr], axis=1),
                                   jnp.concatenate([zr, bf(rhs[b_])], axis=1)], axis=0)
            uw = jnp.concatenate([rhs[a_], rhs[b_]], axis=1) + _dot(bf(n[hk]), rbd)
            for i_, h in enumerate((a_, b_)):
                o0 = 2 * GDN_DIM * i_
                u_scr[wslot, h] = uw[:, o0:o0 + GDN_DIM]
                wq_scr[wslot, h] = jnp.concatenate(
                    [bf(uw[:, o0 + GDN_DIM:o0 + 2 * GDN_DIM]), bf(q[hk] * egc[h])], axis=0)
                kd_scr[wslot, h] = bf(k[hk] * kdf[h])

    _interleave([recurrence(), prep()])


def _gdn(proj3d, ba3d, conv_w, prow, onorm, *, qkv_col, z_col):
    b, t, _ = proj3d.shape
    c = CHUNK
    nt = t // c
    conv_ch = conv_w.shape[1]
    v_width = GDN_V_HEADS * GDN_DIM
    cur = lambda bi, si: (bi, jnp.minimum(si, nt - 1))
    prev = lambda bi, si: (bi, jnp.maximum(si - 1, 0))
    const = lambda bi, si: (0, 0)
    heads = (2, GDN_V_HEADS)
    return pl.pallas_call(
        _gdn_body,
        grid=(b, nt + 1),
        in_specs=[
            pl.BlockSpec((None, c, conv_ch), lambda bi, si: cur(bi, si) + (qkv_col,)),
            pl.BlockSpec((None, c, v_width), lambda bi, si: prev(bi, si) + (z_col,)),
            pl.BlockSpec((None, c, LANES), lambda bi, si: cur(bi, si) + (0,)),
            pl.BlockSpec((CONV_WIDTH, conv_ch), const),
            pl.BlockSpec((8, LANES), const),
            pl.BlockSpec((1, GDN_DIM), const),
        ],
        out_specs=pl.BlockSpec((None, c, v_width), lambda bi, si: prev(bi, si) + (0,)),
        out_shape=jax.ShapeDtypeStruct((b, t, v_width), BF16),
        scratch_shapes=[pltpu.VMEM((conv_ch // LANES, c + 8, LANES), F32),
                        pltpu.VMEM((GDN_V_HEADS, GDN_DIM, GDN_DIM), F32),
                        pltpu.VMEM(heads + (c, GDN_DIM), F32),
                        pltpu.VMEM(heads + (2 * c, GDN_DIM), BF16),
                        pltpu.VMEM(heads + (c, GDN_DIM), BF16),
                        pltpu.VMEM((2, GDN_QK_HEADS, c, 2 * c), BF16),
                        pltpu.VMEM(heads + (8, LANES), F32),
                        pltpu.VMEM((GDN_V_HEADS, c, GDN_DIM), F32)],
        compiler_params=pltpu.CompilerParams(
            dimension_semantics=("parallel", "arbitrary"), vmem_limit_bytes=VMEM_LIMIT),
        name="gdn",
    )(proj3d, proj3d, ba3d, conv_w, prow, onorm)


def _merge_body(x_ref, oa_ref, ob_ref, ga_ref, gb_ref, wa_ref, wb_ref, wo_ref, gain_ref, h_ref):
    ya = _dot(oa_ref[...], wa_ref[...])
    yb = _dot(ob_ref[...], wb_ref[...])
    merged = (jax.nn.sigmoid(ga_ref[...].astype(F32)) * ya
              + jax.nn.sigmoid(gb_ref[...].astype(F32)) * yb)
    mix = _dot(merged.astype(BF16), wo_ref[...])
    h_ref[...] = x_ref[...] + _rms(mix, gain_ref[...])


def _merge(x2d, o_a, o_b, proj, w_a, w_b, w_o, gain, *, ga_col, gb_col, tm=512):
    n, d = x2d.shape
    row = lambda i: (i, 0)
    const = lambda i: (0, 0)
    return pl.pallas_call(
        _merge_body,
        grid=(n // tm,),
        in_specs=[
            pl.BlockSpec((tm, d), row),
            pl.BlockSpec((tm, o_a.shape[1]), row),
            pl.BlockSpec((tm, o_b.shape[1]), row),
            pl.BlockSpec((tm, d), lambda i: (i, ga_col)),
            pl.BlockSpec((tm, d), lambda i: (i, gb_col)),
            pl.BlockSpec(w_a.shape, const),
            pl.BlockSpec(w_b.shape, const),
            pl.BlockSpec(w_o.shape, const),
            pl.BlockSpec((1, d), const),
        ],
        out_specs=pl.BlockSpec((tm, d), row),
        out_shape=jax.ShapeDtypeStruct((n, d), F32),
        compiler_params=pltpu.CompilerParams(
            dimension_semantics=("parallel",), vmem_limit_bytes=VMEM_LIMIT),
        name="merge_out",
    )(x2d, o_a, o_b, proj, proj, w_a, w_b, w_o, gain)


def _mlp_body(h_ref, p_ref, g_pre_ref, wup_ref, wdn_ref, g_post_ref, wple_ref, wgate_ref,
              g_ple_ref, out_ref, *, ff_chunk):
    h = h_ref[...]
    u = _rms(h, g_pre_ref[...]).astype(BF16)
    d_ff = wup_ref.shape[1]
    acc = jnp.zeros(h.shape, F32)
    for c0 in range(0, d_ff, ff_chunk):
        hid = jnp.square(jnp.maximum(_dot(u, wup_ref[:, c0:c0 + ff_chunk]), 0.0))
        acc = acc + _dot(hid.astype(BF16), wdn_ref[c0:c0 + ff_chunk, :])
    h = h + _rms(acc, g_post_ref[...])
    e = _dot(p_ref[...].astype(BF16), wple_ref[...]) * jax.nn.sigmoid(
        _dot(h.astype(BF16), wgate_ref[...]))
    out_ref[...] = h + _rms(e, g_ple_ref[...])


def _mlp(h2d, p2d, g_pre, w_up, w_dn, g_post, w_ple, w_gate, g_ple, *, tm=512, ff_chunk=1024):
    n, d = h2d.shape
    row = lambda i: (i, 0)
    const = lambda i: (0, 0)
    return pl.pallas_call(
        functools.partial(_mlp_body, ff_chunk=ff_chunk),
        grid=(n // tm,),
        in_specs=[
            pl.BlockSpec((tm, d), row),
            pl.BlockSpec((tm, p2d.shape[1]), row),
            pl.BlockSpec((1, d), const),
            pl.BlockSpec(w_up.shape, const),
            pl.BlockSpec(w_dn.shape, const),
            pl.BlockSpec((1, d), const),
            pl.BlockSpec(w_ple.shape, const),
            pl.BlockSpec(w_gate.shape, const),
            pl.BlockSpec((1, d), const),
        ],
        out_specs=pl.BlockSpec((tm, d), row),
        out_shape=jax.ShapeDtypeStruct((n, d), F32),
        compiler_params=pltpu.CompilerParams(
            dimension_semantics=("parallel",), vmem_limit_bytes=VMEM_LIMIT),
        name="mlp_ple",
    )(h2d, p2d, g_pre, w_up, w_dn, g_post, w_ple, w_gate, g_ple)


def _layer(h, p_i, i, pre_mix_norm, w_in, conv_w, lambda_q1, lambda_k1, lambda_q2, lambda_k2,
           da_sub_norm, gdn_a_log, gdn_dt_bias, gdn_out_norm, w_branch_a, w_branch_b, w_out,
           post_mix_norm, pre_mlp_norm, w_up, w_down, post_mlp_norm, w_ple, w_ple_gate, ple_norm):
    b, t, d = h.shape
    n = b * t
    da_w = DA_HEADS * 2 * DA_HEAD_DIM
    gqk_w = GDN_QK_HEADS * GDN_DIM
    gv_w = GDN_V_HEADS * GDN_DIM
    conv_ch = 2 * gqk_w + gv_w
    sizes = (da_w, da_w, da_w, conv_ch, gv_w, GDN_V_HEADS, GDN_V_HEADS, d, d)
    offs = [0]
    for s in sizes:
        offs.append(offs[-1] + s)
    piece = lambda j: w_in[:, offs[j]:offs[j + 1]]
    w_main = jnp.concatenate([piece(3), piece(4), piece(0) * math.log2(math.e), piece(1), piece(2),
                              piece(7), piece(8)],
                             axis=1).astype(BF16)
    w_ba = jnp.pad(jnp.concatenate([piece(5), piece(6)], axis=1),
                   ((0, 0), (0, LANES - 2 * GDN_V_HEADS))).astype(BF16)
    col = {"conv": 0, "z": conv_ch, "q": conv_ch + gv_w, "k": conv_ch + gv_w + da_w,
           "v": conv_ch + gv_w + 2 * da_w, "ga": conv_ch + gv_w + 3 * da_w,
           "gb": conv_ch + gv_w + 3 * da_w + d}

    x2d = h.reshape(n, d)
    proj, ba = _in_proj(x2d, pre_mix_norm.reshape(1, d), w_main, w_ba)
    proj3d = proj.reshape(b, t, -1)

    lambda_init = 0.8 - 0.6 * math.exp(-0.3 * i)
    o_a = _diff_attn(proj3d, lambda_q1.reshape(1, -1), lambda_k1.reshape(1, -1),
                     lambda_q2.reshape(1, -1), lambda_k2.reshape(1, -1), da_sub_norm.reshape(1, -1),
                     q_col=col["q"] // LANES, k_col=col["k"] // LANES, v_col=col["v"] // LANES,
                     lambda_init=lambda_init)

    lanes = jnp.zeros((LANES,), F32)
    a_log = lanes.at[GDN_V_HEADS:2 * GDN_V_HEADS].set(gdn_a_log.astype(F32))
    dt_b = lanes.at[GDN_V_HEADS:2 * GDN_V_HEADS].set(gdn_dt_bias.astype(F32))
    prow = jnp.zeros((8, LANES), F32).at[0].set(a_log).at[1].set(dt_b)
    o_b = _gdn(proj3d, ba.reshape(b, t, LANES), conv_w, prow, gdn_out_norm.reshape(1, -1),
               qkv_col=col["conv"] // conv_ch, z_col=col["z"] // gv_w)

    h1 = _merge(x2d, o_a.reshape(n, -1), o_b.reshape(n, -1), proj, w_branch_a.astype(BF16),
                w_branch_b.astype(BF16), w_out.astype(BF16), post_mix_norm.reshape(1, d),
                ga_col=col["ga"] // d, gb_col=col["gb"] // d)
    out = _mlp(h1, p_i.reshape(n, -1), pre_mlp_norm.reshape(1, d), w_up.astype(BF16),
               w_down.astype(BF16), post_mlp_norm.reshape(1, d), w_ple.astype(BF16),
               w_ple_gate.astype(BF16), ple_norm.reshape(1, d))
    return out.reshape(b, t, d)


def kernel(x, p, pre_mix_norm, w_in, conv_w, lambda_q1, lambda_k1, lambda_q2, lambda_k2,
           da_sub_norm, gdn_a_log, gdn_dt_bias, gdn_out_norm, w_branch_a, w_branch_b, w_out,
           post_mix_norm, pre_mlp_norm, w_up, w_down, post_mlp_norm, w_ple, w_ple_gate, ple_norm):
    per_layer = (pre_mix_norm, w_in, conv_w, lambda_q1, lambda_k1, lambda_q2, lambda_k2,
                 da_sub_norm, gdn_a_log, gdn_dt_bias, gdn_out_norm, w_branch_a, w_branch_b, w_out,
                 post_mix_norm, pre_mlp_norm, w_up, w_down, post_mlp_norm, w_ple, w_ple_gate,
                 ple_norm)
    h = x
    for i in range(p.shape[0]):
        h = _layer(h, p[i], i, *(w[i] for w in per_layer))
    return h
```

```python
import functools
import math

import jax
import jax.numpy as jnp
from jax import lax
from jax.experimental import pallas as pl
from jax.experimental.pallas import tpu as pltpu

F32 = jnp.float32
BF16 = jnp.bfloat16

EPS = 1e-6
LANES = 128
DA_HEADS = 8
DA_HEAD_DIM = 64
GDN_QK_HEADS = 8
GDN_V_HEADS = 16
GDN_DIM = 128
CONV_WIDTH = 4
CHUNK = 64
GDN_CPB = 2
ATTN_CHAINS = 4
NEG = -1e30
VMEM_LIMIT = 56 * 1024 * 1024

NT_DIMS = (((1,), (1,)), ((), ()))
TN_DIMS = (((0,), (0,)), ((), ()))


def _rms(x, gain):
    return x * lax.rsqrt(jnp.mean(x * x, axis=-1, keepdims=True) + EPS) * gain


def _dot(a, b):
    return jnp.dot(a, b, preferred_element_type=F32)


def _inproj_body(x_ref, gain_ref, w_ref, wba_ref, out_ref, ba_ref, u_scr):
    @pl.when(pl.program_id(1) == 0)
    def _():
        u = _rms(x_ref[...], gain_ref[...]).astype(BF16)
        u_scr[...] = u
        ba_ref[...] = _dot(u, wba_ref[...])

    out_ref[...] = _dot(u_scr[...], w_ref[...]).astype(out_ref.dtype)


def _in_proj(x2d, gain, w_main, w_ba, *, tm=1024, tn=1024):
    n, d = x2d.shape
    width = w_main.shape[1]
    return pl.pallas_call(
        _inproj_body,
        grid=(n // tm, width // tn),
        in_specs=[
            pl.BlockSpec((tm, d), lambda i, j: (i, 0)),
            pl.BlockSpec((1, d), lambda i, j: (0, 0)),
            pl.BlockSpec((d, tn), lambda i, j: (0, j)),
            pl.BlockSpec((d, LANES), lambda i, j: (0, 0)),
        ],
        out_specs=[
            pl.BlockSpec((tm, tn), lambda i, j: (i, j)),
            pl.BlockSpec((tm, LANES), lambda i, j: (i, 0)),
        ],
        out_shape=[
            jax.ShapeDtypeStruct((n, width), BF16),
            jax.ShapeDtypeStruct((n, LANES), F32),
        ],
        scratch_shapes=[pltpu.VMEM((tm, d), BF16)],
        compiler_params=pltpu.CompilerParams(
            dimension_semantics=("parallel", "arbitrary"), vmem_limit_bytes=VMEM_LIMIT),
        name="in_proj",
    )(x2d, gain, w_main, w_ba)


def _interleave(chains):
    chains = list(chains)
    while chains:
        for g in list(chains):
            if next(g, StopIteration) is StopIteration:
                chains.remove(g)


def _attn_body(lq1_ref, lk1_ref, lq2_ref, lk2_ref, gain_ref, q_ref, k_ref, v_ref, o_ref,
               qt_scr, vt_scr, *, tq, lambda_init):
    t = q_ref.shape[0]
    nq = t // tq
    scale = jnp.asarray(DA_HEAD_DIM ** -0.5, BF16)
    for i in range(nq):
        sl = slice(i * tq, (i + 1) * tq)
        qt_scr[:, sl] = (q_ref[sl, :] * scale).T
        vt_scr[0:LANES, sl] = v_ref[sl, :].T
    vt_scr[LANES:, :] = jnp.ones((vt_scr.shape[0] - LANES, t), BF16)
    lam = (jnp.exp(jnp.sum(lq1_ref[...] * lk1_ref[...], axis=-1, keepdims=True))
           - jnp.exp(jnp.sum(lq2_ref[...] * lk2_ref[...], axis=-1, keepdims=True)) + lambda_init)
    feat = lax.broadcasted_iota(jnp.int32, (LANES, tq), 0)
    kv_row = lax.broadcasted_iota(jnp.int32, (tq, 2 * tq), 0)
    q_col = lax.broadcasted_iota(jnp.int32, (tq, 2 * tq), 1)
    on_or_below_diag = kv_row <= jnp.where(q_col >= tq, q_col - tq, q_col)

    def chain(tiles):
        for qi in tiles:
            qsl = slice(qi * tq, (qi + 1) * tq)
            qt = qt_scr[:, qsl]
            zero = jnp.zeros_like(qt)
            qs = jnp.concatenate([jnp.where(feat < DA_HEAD_DIM, qt, zero),
                                  jnp.where(feat >= DA_HEAD_DIM, qt, zero)], axis=1)
            scores = lambda j: _dot(k_ref[j * tq:(j + 1) * tq, :], qs)
            m = jnp.full((1, 2 * tq), NEG, F32)
            acc = jnp.zeros((vt_scr.shape[0], 2 * tq), F32)
            s_next = scores(0)
            yield
            for j in range(qi + 1):
                s = s_next
                if j < qi:
                    s_next = scores(j + 1)
                    yield
                else:
                    s = jnp.where(on_or_below_diag, s, NEG)
                m_new = jnp.maximum(m, jnp.max(s, axis=0, keepdims=True))
                alpha = jnp.exp2(m - m_new)
                p = jnp.exp2(s - m_new)
                pv = _dot(vt_scr[:, j * tq:(j + 1) * tq], p.astype(BF16))
                yield
                acc = alpha * acc + pv
                m = m_new
            on = acc[0:LANES] * (1.0 / acc[LANES:LANES + 1])
            o = on[:, :tq] - lam * on[:, tq:]
            y = o * lax.rsqrt(jnp.mean(o * o, axis=0, keepdims=True) + EPS)
            o_ref[qsl, :] = (y.T * gain_ref[...] * (1.0 - lambda_init)).astype(o_ref.dtype)

    pairs = [[nq - 1 - i, i] for i in range(nq // 2)]
    for c0 in range(0, len(pairs), ATTN_CHAINS):
        _interleave(chain(tiles) for tiles in pairs[c0:c0 + ATTN_CHAINS])


def _diff_attn(proj3d, lq1, lk1, lq2, lk2, sub_gain, *, q_col, k_col, v_col, lambda_init, tq=256):
    b, t, _ = proj3d.shape
    vec = pl.BlockSpec((1, DA_HEAD_DIM), lambda bi, h: (0, 0))
    head = lambda col: pl.BlockSpec((None, t, LANES), lambda bi, h: (bi, 0, col + h))
    return pl.pallas_call(
        functools.partial(_attn_body, tq=tq, lambda_init=lambda_init),
        grid=(b, DA_HEADS),
        in_specs=[vec, vec, vec, vec, pl.BlockSpec((1, LANES), lambda bi, h: (0, 0)),
                  head(q_col), head(k_col), head(v_col)],
        out_specs=pl.BlockSpec((None, t, LANES), lambda bi, h: (bi, 0, h)),
        out_shape=jax.ShapeDtypeStruct((b, t, DA_HEADS * LANES), BF16),
        scratch_shapes=[pltpu.VMEM((LANES, t), BF16), pltpu.VMEM((LANES + 16, t), BF16)],
        compiler_params=pltpu.CompilerParams(
            dimension_semantics=("parallel", "parallel"), vmem_limit_bytes=VMEM_LIMIT),
        name="diff_attn",
    )(lq1, lk1, lq2, lk2, sub_gain, proj3d, proj3d, proj3d)


def _softplus(x):
    return jnp.maximum(x, 0.0) + jnp.log1p(jnp.exp(-jnp.abs(x)))


def _gdn_body(qkv_ref, z_ref, ba_ref, convw_ref, prow_ref, onorm_ref, o_ref,
              xx_scr, state_scr, u_scr, wq_scr, kd_scr, intra_scr, eg_scr, out_scr):
    c = CHUNK
    rows = qkv_ref.shape[0]
    cis = list(range(rows // c))
    step = pl.program_id(1)
    wslot = step % 2
    rslot = 1 - wslot
    rep = GDN_V_HEADS // GDN_QK_HEADS
    hks = [(ci, hk) for ci in cis for hk in range(GDN_QK_HEADS)]
    hs = [(ci, h) for ci in cis for h in range(GDN_V_HEADS)]
    fh = lambda ci, h: ci * GDN_V_HEADS + h
    fk = lambda ci, hk: ci * GDN_QK_HEADS + hk
    bf = lambda y: y.astype(BF16)

    @pl.when(step == 0)
    def _():
        xx_scr[:, 0:8, :] = jnp.zeros((xx_scr.shape[0], 8, LANES), F32)
        state_scr[...] = jnp.zeros(state_scr.shape, F32)
        u_scr[1] = jnp.zeros(u_scr.shape[1:], u_scr.dtype)
        wq_scr[1] = jnp.zeros(wq_scr.shape[1:], wq_scr.dtype)
        kd_scr[1] = jnp.zeros(kd_scr.shape[1:], kd_scr.dtype)
        intra_scr[1] = jnp.zeros(intra_scr.shape[1:], intra_scr.dtype)
        eg_scr[1] = jnp.zeros(eg_scr.shape[1:], eg_scr.dtype)

    @pl.when(step > 0)
    def _():
        xx_scr[:, 0:8, :] = xx_scr[:, rows:rows + 8, :]

    for slab in range(xx_scr.shape[0]):
        xx_scr[slab, 8:8 + rows, :] = qkv_ref[:, slab * LANES:(slab + 1) * LANES].astype(F32)

    def row_blocks():
        return [(32 * g + r) for g in range(c // 32) for r in range(4)]

    def perm_time(i):
        return ((i >> 5) << 5) + ((i & 7) << 2) + ((i >> 3) & 3)

    def conv_silu(ci, slab):
        sl = slice(slab * LANES, (slab + 1) * LANES)
        w = [convw_ref[k:k + 1, sl] for k in range(CONV_WIDTH)]
        outs = []
        for g in range(c // 32):
            z = {rho: xx_scr[slab, pl.ds(8 + c * ci + 32 * g + rho, 8, stride=4), :]
                 for rho in range(1 - CONV_WIDTH, 4)}
            for r in range(4):
                y = w[CONV_WIDTH - 1] * z[r]
                for d in range(1, CONV_WIDTH):
                    y = y + w[CONV_WIDTH - 1 - d] * z[r - d]
                outs.append(y)
        y = jnp.concatenate(outs, axis=0)
        return y * jax.nn.sigmoid(y)

    def l2n(y):
        return y * lax.rsqrt(jnp.sum(y * y, axis=-1, keepdims=True) + EPS)

    lane = lax.broadcasted_iota(jnp.int32, (c, 2 * c), 1)
    first = lane < c
    zeros_c = jnp.zeros((c, 2 * c), BF16)

    def blockdiag(x):
        xb = bf(x)
        return jnp.concatenate([jnp.where(first, xb, zeros_c), jnp.where(first, zeros_c, xb)], axis=0)

    def recurrence():
        s = {h: state_scr[h] for h in range(GDN_V_HEADS)}
        o = {}
        for ci in cis:
            heads = range(GDN_V_HEADS)
            ws = {h: _dot(wq_scr[rslot, fh(ci, h)], bf(s[h])) for h in heads}
            yield
            vnb = {h: bf(u_scr[rslot, fh(ci, h)] - ws[h][:c]) for h in heads}
            zv = jnp.zeros((c, GDN_DIM), BF16)
            for hk in range(GDN_QK_HEADS):
                a_, b_ = rep * hk, rep * hk + 1
                vbd = jnp.concatenate([jnp.concatenate([vnb[a_], zv], axis=1),
                                       jnp.concatenate([zv, vnb[b_]], axis=1)], axis=0)
                ov = _dot(intra_scr[rslot, fk(ci, hk)], vbd)
                o[ci, a_] = ws[a_][c:] + ov[:, :GDN_DIM]
                o[ci, b_] = ws[b_][c:] + ov[:, GDN_DIM:]
            s = {h: s[h] * eg_scr[rslot, fh(ci, h)][0:1, :] + lax.dot_general(
                kd_scr[rslot, fh(ci, h)], vnb[h], TN_DIMS, preferred_element_type=F32) for h in heads}
            yield
        for h in range(GDN_V_HEADS):
            state_scr[h] = s[h]
        for ci, h in hs:
            y = _rms(o[ci, h], onorm_ref[...])
            for m, row0 in enumerate(row_blocks()):
                out_scr[fh(ci, h), pl.ds(row0, 8, stride=4), :] = y[8 * m:8 * m + 8]
        for ci, h in hs:
            sl = slice(h * GDN_DIM, (h + 1) * GDN_DIM)
            z = z_ref[c * ci:c * (ci + 1), sl].astype(F32)
            o_ref[c * ci:c * (ci + 1), sl] = (out_scr[fh(ci, h)] * (z * jax.nn.sigmoid(z))
                                              ).astype(o_ref.dtype)

    def prep():
        ii = perm_time(lax.broadcasted_iota(jnp.int32, (c, c), 0))
        jj = perm_time(lax.broadcasted_iota(jnp.int32, (c, c), 1))
        row2 = perm_time(lax.broadcasted_iota(jnp.int32, (c, 2 * c), 0))
        col2 = perm_time(jnp.where(first, lane, lane - c))
        strict = row2 > col2
        causal = row2 >= col2
        hp = lax.Precision.HIGHEST
        dup = lambda m: jnp.concatenate([m, m], axis=1)
        tril = jnp.where(ii >= jj, 1.0, 0.0).astype(F32)
        triu2 = dup(jnp.where(ii <= jj, 1.0, 0.0).astype(F32))
        eye2 = dup(jnp.where(ii == jj, 1.0, 0.0).astype(F32))
        gc_row, beta_row, gcol, egc, kdf = {}, {}, {}, {}, {}
        for ci in cis:
            ba = jnp.concatenate([ba_ref[pl.ds(c * ci + row0, 8, stride=4), :]
                                  for row0 in row_blocks()], axis=0)
            beta_col = jax.nn.sigmoid(ba)
            g_col = -jnp.exp(prow_ref[0:1, :]) * _softplus(ba + prow_ref[1:2, :])
            gc_col = jnp.dot(tril, g_col, precision=hp, preferred_element_type=F32)
            gc_row[ci] = lax.dot_general(g_col, triu2, TN_DIMS, precision=hp,
                                         preferred_element_type=F32)
            beta_row[ci] = lax.dot_general(beta_col, eye2, TN_DIMS, precision=hp,
                                           preferred_element_type=F32)
            g_last = gc_col[c - 1:c, :]
            egc_all = jnp.exp(gc_col)
            kdf_all = pltpu.roll(beta_col, GDN_V_HEADS, axis=1) * jnp.exp(g_last - gc_col)
            eg_all = jnp.exp(g_last)
            for h in range(GDN_V_HEADS):
                ln = slice(GDN_V_HEADS + h, GDN_V_HEADS + h + 1)
                gcol[ci, h] = jnp.broadcast_to(gc_col[:, ln], (c, 2 * c))
                egc[ci, h] = jnp.broadcast_to(egc_all[:, ln], (c, GDN_DIM))
                kdf[ci, h] = jnp.broadcast_to(kdf_all[:, ln], (c, GDN_DIM))
                eg_scr[wslot, fh(ci, h)] = jnp.broadcast_to(eg_all[:, ln], eg_scr.shape[2:])
        q = {(ci, hk): l2n(conv_silu(ci, hk)) * (GDN_DIM ** -0.5) for ci, hk in hks}
        k = {(ci, hk): l2n(conv_silu(ci, GDN_QK_HEADS + hk)) for ci, hk in hks}
        gram = {x: lax.dot_general(jnp.concatenate([bf(k[x]), bf(q[x])], axis=0),
                                   jnp.concatenate([bf(k[x]), bf(k[x])], axis=0),
                                   NT_DIMS, preferred_element_type=F32) for x in hks}
        yield
        pick = lambda m, hk: jnp.where(first[0:1, :], m[rep * hk:rep * hk + 1, :],
                                       m[rep * hk + 1:rep * hk + 2, :])
        a = {}
        for ci, hk in hks:
            gcol2 = jnp.where(first, gcol[ci, rep * hk], gcol[ci, rep * hk + 1])
            dec = jnp.where(causal, jnp.exp(gcol2 - pick(gc_row[ci][GDN_V_HEADS:], hk)), 0.0)
            brow = pick(beta_row[ci], hk)
            a[ci, hk] = jnp.where(strict, gram[ci, hk][:c] * dec, 0.0) * brow
            intra_scr[wslot, fk(ci, hk)] = bf(gram[ci, hk][c:] * dec * brow)
        n = {x: -a[x] for x in hks}
        p = {x: _dot(bf(a[x]), blockdiag(a[x])) for x in hks}
        yield
        for _ in range(4):
            r_ = {x: _dot(jnp.concatenate([bf(n[x]), bf(p[x])], axis=0), blockdiag(p[x])) for x in hks}
            yield
            n = {x: n[x] + p[x] + r_[x][:c] for x in hks}
            p = {x: r_[x][c:] for x in hks}
        n = {x: n[x] + p[x] + _dot(bf(n[x]), blockdiag(p[x])) for x in hks}
        yield
        rhs = {(ci, h): jnp.concatenate([conv_silu(ci, 2 * GDN_QK_HEADS + h),
                                         k[ci, h // rep] * egc[ci, h]], axis=1) for ci, h in hs}
        zr = jnp.zeros((c, 2 * GDN_DIM), BF16)
        for ci, hk in hks:
            a_, b_ = rep * hk, rep * hk + 1
            rbd = jnp.concatenate([jnp.concatenate([bf(rhs[ci, a_]), zr], axis=1),
                                   jnp.concatenate([zr, bf(rhs[ci, b_])], axis=1)], axis=0)
            uw = (jnp.concatenate([rhs[ci, a_], rhs[ci, b_]], axis=1)
                  + _dot(bf(n[ci, hk]), rbd))
            for i_, h in enumerate((a_, b_)):
                o0 = 2 * GDN_DIM * i_
                u_scr[wslot, fh(ci, h)] = uw[:, o0:o0 + GDN_DIM]
                wq_scr[wslot, fh(ci, h)] = jnp.concatenate(
                    [bf(uw[:, o0 + GDN_DIM:o0 + 2 * GDN_DIM]), bf(q[ci, hk] * egc[ci, h])], axis=0)
                kd_scr[wslot, fh(ci, h)] = bf(k[ci, hk] * kdf[ci, h])

    _interleave([recurrence(), prep()])


def _gdn(proj3d, ba3d, conv_w, prow, onorm, *, qkv_col, z_col):
    b, t, _ = proj3d.shape
    c = CHUNK
    rows = c * GDN_CPB
    nt = t // rows
    conv_ch = conv_w.shape[1]
    v_width = GDN_V_HEADS * GDN_DIM
    cur = lambda bi, si: (bi, jnp.minimum(si, nt - 1))
    prev = lambda bi, si: (bi, jnp.maximum(si - 1, 0))
    const = lambda bi, si: (0, 0)
    heads = (2, GDN_CPB * GDN_V_HEADS)
    return pl.pallas_call(
        _gdn_body,
        grid=(b, nt + 1),
        in_specs=[
            pl.BlockSpec((None, rows, conv_ch), lambda bi, si: cur(bi, si) + (qkv_col,)),
            pl.BlockSpec((None, rows, v_width), lambda bi, si: prev(bi, si) + (z_col,)),
            pl.BlockSpec((None, rows, LANES), lambda bi, si: cur(bi, si) + (0,)),
            pl.BlockSpec((CONV_WIDTH, conv_ch), const),
            pl.BlockSpec((8, LANES), const),
            pl.BlockSpec((1, GDN_DIM), const),
        ],
        out_specs=pl.BlockSpec((None, rows, v_width), lambda bi, si: prev(bi, si) + (0,)),
        out_shape=jax.ShapeDtypeStruct((b, t, v_width), BF16),
        scratch_shapes=[pltpu.VMEM((conv_ch // LANES, rows + 8, LANES), F32),
                        pltpu.VMEM((GDN_V_HEADS, GDN_DIM, GDN_DIM), F32),
                        pltpu.VMEM(heads + (c, GDN_DIM), F32),
                        pltpu.VMEM(heads + (2 * c, GDN_DIM), BF16),
                        pltpu.VMEM(heads + (c, GDN_DIM), BF16),
                        pltpu.VMEM((2, GDN_CPB * GDN_QK_HEADS, c, 2 * c), BF16),
                        pltpu.VMEM(heads + (8, LANES), F32),
                        pltpu.VMEM(heads[1:] + (c, GDN_DIM), F32)],
        compiler_params=pltpu.CompilerParams(
            dimension_semantics=("parallel", "arbitrary"), vmem_limit_bytes=VMEM_LIMIT),
        name="gdn",
    )(proj3d, proj3d, ba3d, conv_w, prow, onorm)


def _post_body(x_ref, oa_ref, ob_ref, ga_ref, gb_ref, p_ref, wa_ref, wb_ref, wo_ref, g_mix_ref,
               g_pre_ref, wup_ref, wdn_ref, g_post_ref, wple_ref, wgate_ref, g_ple_ref, out_ref,
               *, ff_chunk):
    ya = _dot(oa_ref[...], wa_ref[...])
    yb = _dot(ob_ref[...], wb_ref[...])
    merged = (jax.nn.sigmoid(ga_ref[...].astype(F32)) * ya
              + jax.nn.sigmoid(gb_ref[...].astype(F32)) * yb)
    h = x_ref[...] + _rms(_dot(merged.astype(BF16), wo_ref[...]), g_mix_ref[...])
    u = _rms(h, g_pre_ref[...]).astype(BF16)
    d_ff = wup_ref.shape[1]
    acc = jnp.zeros(h.shape, F32)
    for c0 in range(0, d_ff, ff_chunk):
        hid = jnp.square(jnp.maximum(_dot(u, wup_ref[:, c0:c0 + ff_chunk]), 0.0))
        acc = acc + _dot(hid.astype(BF16), wdn_ref[c0:c0 + ff_chunk, :])
    h = h + _rms(acc, g_post_ref[...])
    e = _dot(p_ref[...].astype(BF16), wple_ref[...]) * jax.nn.sigmoid(
        _dot(h.astype(BF16), wgate_ref[...]))
    out_ref[...] = h + _rms(e, g_ple_ref[...])


def _post(x2d, o_a, o_b, proj, p2d, w_a, w_b, w_o, g_mix, g_pre, w_up, w_dn, g_post, w_ple, w_gate,
          g_ple, *, ga_col, gb_col, tm=512, ff_chunk=1024):
    n, d = x2d.shape
    row = lambda i: (i, 0)
    const = lambda i: (0, 0)
    weight = lambda w: pl.BlockSpec(w.shape, const, pipeline_mode=pl.Buffered(1))
    gain = pl.BlockSpec((1, d), const)
    return pl.pallas_call(
        functools.partial(_post_body, ff_chunk=ff_chunk),
        grid=(n // tm,),
        in_specs=[
            pl.BlockSpec((tm, d), row),
            pl.BlockSpec((tm, o_a.shape[1]), row),
            pl.BlockSpec((tm, o_b.shape[1]), row),
            pl.BlockSpec((tm, d), lambda i: (i, ga_col)),
            pl.BlockSpec((tm, d), lambda i: (i, gb_col)),
            pl.BlockSpec((tm, p2d.shape[1]), row),
            weight(w_a), weight(w_b), weight(w_o), gain,
            gain, weight(w_up), weight(w_dn), gain,
            weight(w_ple), weight(w_gate), gain,
        ],
        out_specs=pl.BlockSpec((tm, d), row),
        out_shape=jax.ShapeDtypeStruct((n, d), F32),
        compiler_params=pltpu.CompilerParams(
            dimension_semantics=("parallel",), vmem_limit_bytes=VMEM_LIMIT),
        name="post",
    )(x2d, o_a, o_b, proj, proj, p2d, w_a, w_b, w_o, g_mix, g_pre, w_up, w_dn, g_post, w_ple,
      w_gate, g_ple)


def _layer(h, p_i, i, pre_mix_norm, w_in, conv_w, lambda_q1, lambda_k1, lambda_q2, lambda_k2,
           da_sub_norm, gdn_a_log, gdn_dt_bias, gdn_out_norm, w_branch_a, w_branch_b, w_out,
           post_mix_norm, pre_mlp_norm, w_up, w_down, post_mlp_norm, w_ple, w_ple_gate, ple_norm):
    b, t, d = h.shape
    n = b * t
    da_w = DA_HEADS * 2 * DA_HEAD_DIM
    gqk_w = GDN_QK_HEADS * GDN_DIM
    gv_w = GDN_V_HEADS * GDN_DIM
    conv_ch = 2 * gqk_w + gv_w
    sizes = (da_w, da_w, da_w, conv_ch, gv_w, GDN_V_HEADS, GDN_V_HEADS, d, d)
    offs = [0]
    for s in sizes:
        offs.append(offs[-1] + s)
    col_scale = jnp.ones((offs[-1],), F32).at[:da_w].set(math.log2(math.e))
    w_bf = (w_in * col_scale).astype(BF16)
    piece = lambda j: w_bf[:, offs[j]:offs[j + 1]]
    w_main = jnp.concatenate([piece(3), piece(4), piece(0), piece(1), piece(2), piece(7), piece(8)],
                             axis=1)
    w_ba = jnp.pad(jnp.concatenate([piece(5), piece(6)], axis=1),
                   ((0, 0), (0, LANES - 2 * GDN_V_HEADS)))
    col = {"conv": 0, "z": conv_ch, "q": conv_ch + gv_w, "k": conv_ch + gv_w + da_w,
           "v": conv_ch + gv_w + 2 * da_w, "ga": conv_ch + gv_w + 3 * da_w,
           "gb": conv_ch + gv_w + 3 * da_w + d}

    x2d = h.reshape(n, d)
    proj, ba = _in_proj(x2d, pre_mix_norm.reshape(1, d), w_main, w_ba)
    proj3d = proj.reshape(b, t, -1)

    lambda_init = 0.8 - 0.6 * math.exp(-0.3 * i)
    o_a = _diff_attn(proj3d, lambda_q1.reshape(1, -1), lambda_k1.reshape(1, -1),
                     lambda_q2.reshape(1, -1), lambda_k2.reshape(1, -1), da_sub_norm.reshape(1, -1),
                     q_col=col["q"] // LANES, k_col=col["k"] // LANES, v_col=col["v"] // LANES,
                     lambda_init=lambda_init)

    lanes = jnp.zeros((LANES,), F32)
    a_log = lanes.at[GDN_V_HEADS:2 * GDN_V_HEADS].set(gdn_a_log.astype(F32))
    dt_b = lanes.at[GDN_V_HEADS:2 * GDN_V_HEADS].set(gdn_dt_bias.astype(F32))
    prow = jnp.zeros((8, LANES), F32).at[0].set(a_log).at[1].set(dt_b)
    o_b = _gdn(proj3d, ba.reshape(b, t, LANES), conv_w, prow, gdn_out_norm.reshape(1, -1),
               qkv_col=col["conv"] // conv_ch, z_col=col["z"] // gv_w)

    out = _post(x2d, o_a.reshape(n, -1), o_b.reshape(n, -1), proj, p_i.reshape(n, -1),
                w_branch_a.astype(BF16), w_branch_b.astype(BF16), w_out.astype(BF16),
                post_mix_norm.reshape(1, d), pre_mlp_norm.reshape(1, d), w_up.astype(BF16),
                w_down.astype(BF16), post_mlp_norm.reshape(1, d), w_ple.astype(BF16),
                w_ple_gate.astype(BF16), ple_norm.reshape(1, d),
                ga_col=col["ga"] // d, gb_col=col["gb"] // d)
    return out.reshape(b, t, d)


def kernel(x, p, pre_mix_norm, w_in, conv_w, lambda_q1, lambda_k1, lambda_q2, lambda_k2,
           da_sub_norm, gdn_a_log, gdn_dt_bias, gdn_out_norm, w_branch_a, w_branch_b, w_out,
           post_mix_norm, pre_mlp_norm, w_up, w_down, post_mlp_norm, w_ple, w_ple_gate, ple_norm):
    per_layer = (pre_mix_norm, w_in, conv_w, lambda_q1, lambda_k1, lambda_q2, lambda_k2,
                 da_sub_norm, gdn_a_log, gdn_dt_bias, gdn_out_norm, w_branch_a, w_branch_b, w_out,
                 post_mix_norm, pre_mlp_norm, w_up, w_down, post_mlp_norm, w_ple, w_ple_gate,
                 ple_norm)
    h = x
    for i in range(p.shape[0]):
        h = _layer(h, p[i], i, *(w[i] for w in per_layer))
    return h
```

```python
import functools
import math

import jax
import jax.numpy as jnp
from jax import lax
from jax.experimental import pallas as pl
from jax.experimental.pallas import tpu as pltpu

F32 = jnp.float32
BF16 = jnp.bfloat16

EPS = 1e-6
LANES = 128
DA_HEADS = 8
DA_HEAD_DIM = 64
GDN_QK_HEADS = 8
GDN_V_HEADS = 16
GDN_DIM = 128
CONV_WIDTH = 4
CHUNK = 64
GDN_CPB = 2
ATTN_CHAINS = 4
NEG = -1e30
VMEM_LIMIT = 56 * 1024 * 1024

NT_DIMS = (((1,), (1,)), ((), ()))
TN_DIMS = (((0,), (0,)), ((), ()))


def _rms(x, gain):
    return x * lax.rsqrt(jnp.mean(x * x, axis=-1, keepdims=True) + EPS) * gain


def _dot(a, b):
    return jnp.dot(a, b, preferred_element_type=F32)


def _inproj_body(x_ref, gain_ref, w_ref, wba_ref, out_ref, ba_ref, u_scr):
    @pl.when(pl.program_id(1) == 0)
    def _():
        u = _rms(x_ref[...], gain_ref[...]).astype(BF16)
        u_scr[...] = u
        ba_ref[...] = lax.dot_general(u, wba_ref[...], NT_DIMS, preferred_element_type=F32)

    out_ref[...] = lax.dot_general(u_scr[...], w_ref[...], NT_DIMS,
                                   preferred_element_type=F32).astype(out_ref.dtype)


def _in_proj(x2d, gain, w_main_t, w_ba_t, *, tm=1024, tn=1024):
    n, d = x2d.shape
    width = w_main_t.shape[0]
    return pl.pallas_call(
        _inproj_body,
        grid=(n // tm, width // tn),
        in_specs=[
            pl.BlockSpec((tm, d), lambda i, j: (i, 0)),
            pl.BlockSpec((1, d), lambda i, j: (0, 0)),
            pl.BlockSpec((tn, d), lambda i, j: (j, 0)),
            pl.BlockSpec((LANES, d), lambda i, j: (0, 0)),
        ],
        out_specs=[
            pl.BlockSpec((tm, tn), lambda i, j: (i, j)),
            pl.BlockSpec((tm, LANES), lambda i, j: (i, 0)),
        ],
        out_shape=[
            jax.ShapeDtypeStruct((n, width), BF16),
            jax.ShapeDtypeStruct((n, LANES), F32),
        ],
        scratch_shapes=[pltpu.VMEM((tm, d), BF16)],
        compiler_params=pltpu.CompilerParams(
            dimension_semantics=("parallel", "arbitrary"), vmem_limit_bytes=VMEM_LIMIT),
        name="in_proj",
    )(x2d, gain, w_main_t, w_ba_t)


def _interleave(chains):
    chains = list(chains)
    while chains:
        for g in list(chains):
            if next(g, StopIteration) is StopIteration:
                chains.remove(g)


def _attn_body(lq1_ref, lk1_ref, lq2_ref, lk2_ref, gain_ref, q_ref, k_ref, v_ref, o_ref,
               qt_scr, vt_scr, *, tq, lambda_init):
    t = q_ref.shape[0]
    nq = t // tq
    scale = jnp.asarray(DA_HEAD_DIM ** -0.5, BF16)
    for i in range(nq):
        sl = slice(i * tq, (i + 1) * tq)
        qt_scr[:, sl] = (q_ref[sl, :] * scale).T
        vt_scr[0:LANES, sl] = v_ref[sl, :].T
    vt_scr[LANES:, :] = jnp.ones((vt_scr.shape[0] - LANES, t), BF16)
    lam = (jnp.exp(jnp.sum(lq1_ref[...] * lk1_ref[...], axis=-1, keepdims=True))
           - jnp.exp(jnp.sum(lq2_ref[...] * lk2_ref[...], axis=-1, keepdims=True)) + lambda_init)
    feat = lax.broadcasted_iota(jnp.int32, (LANES, tq), 0)
    kv_row = lax.broadcasted_iota(jnp.int32, (tq, 2 * tq), 0)
    q_col = lax.broadcasted_iota(jnp.int32, (tq, 2 * tq), 1)
    on_or_below_diag = kv_row <= jnp.where(q_col >= tq, q_col - tq, q_col)

    def chain(tiles):
        for qi in tiles:
            qsl = slice(qi * tq, (qi + 1) * tq)
            qt = qt_scr[:, qsl]
            zero = jnp.zeros_like(qt)
            qs = jnp.concatenate([jnp.where(feat < DA_HEAD_DIM, qt, zero),
                                  jnp.where(feat >= DA_HEAD_DIM, qt, zero)], axis=1)
            scores = lambda j: _dot(k_ref[j * tq:(j + 1) * tq, :], qs)
            m = jnp.full((1, 2 * tq), NEG, F32)
            acc = jnp.zeros((vt_scr.shape[0], 2 * tq), F32)
            s_next = scores(0)
            yield
            for j in range(qi + 1):
                s = s_next
                if j < qi:
                    s_next = scores(j + 1)
                    yield
                else:
                    s = jnp.where(on_or_below_diag, s, NEG)
                m_new = jnp.maximum(m, jnp.max(s, axis=0, keepdims=True))
                alpha = jnp.exp2(m - m_new)
                p = jnp.exp2(s - m_new)
                pv = _dot(vt_scr[:, j * tq:(j + 1) * tq], p.astype(BF16))
                yield
                acc = alpha * acc + pv
                m = m_new
            on = acc[0:LANES] * (1.0 / acc[LANES:LANES + 1])
            o = on[:, :tq] - lam * on[:, tq:]
            y = o * lax.rsqrt(jnp.mean(o * o, axis=0, keepdims=True) + EPS)
            o_ref[qsl, :] = (y.T * gain_ref[...] * (1.0 - lambda_init)).astype(o_ref.dtype)

    pairs = [[nq - 1 - i, i] for i in range(nq // 2)]
    for c0 in range(0, len(pairs), ATTN_CHAINS):
        _interleave(chain(tiles) for tiles in pairs[c0:c0 + ATTN_CHAINS])


def _diff_attn(proj3d, lq1, lk1, lq2, lk2, sub_gain, *, q_col, k_col, v_col, lambda_init, tq=256):
    b, t, _ = proj3d.shape
    vec = pl.BlockSpec((1, DA_HEAD_DIM), lambda bi, h: (0, 0))
    head = lambda col: pl.BlockSpec((None, t, LANES), lambda bi, h: (bi, 0, col + h))
    return pl.pallas_call(
        functools.partial(_attn_body, tq=tq, lambda_init=lambda_init),
        grid=(b, DA_HEADS),
        in_specs=[vec, vec, vec, vec, pl.BlockSpec((1, LANES), lambda bi, h: (0, 0)),
                  head(q_col), head(k_col), head(v_col)],
        out_specs=pl.BlockSpec((None, t, LANES), lambda bi, h: (bi, 0, h)),
        out_shape=jax.ShapeDtypeStruct((b, t, DA_HEADS * LANES), BF16),
        scratch_shapes=[pltpu.VMEM((LANES, t), BF16), pltpu.VMEM((LANES + 16, t), BF16)],
        compiler_params=pltpu.CompilerParams(
            dimension_semantics=("parallel", "parallel"), vmem_limit_bytes=VMEM_LIMIT),
        name="diff_attn",
    )(lq1, lk1, lq2, lk2, sub_gain, proj3d, proj3d, proj3d)


def _softplus(x):
    return jnp.maximum(x, 0.0) + jnp.log1p(jnp.exp(-jnp.abs(x)))


def _gdn_body(qkv_ref, z_ref, ba_ref, convw_ref, prow_ref, onorm_ref, o_ref,
              xx_scr, state_scr, u_scr, wq_scr, kd_scr, intra_scr, eg_scr, out_scr):
    c = CHUNK
    rows = qkv_ref.shape[0]
    cis = list(range(rows // c))
    step = pl.program_id(1)
    wslot = step % 2
    rslot = 1 - wslot
    rep = GDN_V_HEADS // GDN_QK_HEADS
    hks = [(ci, hk) for ci in cis for hk in range(GDN_QK_HEADS)]
    hs = [(ci, h) for ci in cis for h in range(GDN_V_HEADS)]
    fh = lambda ci, h: ci * GDN_V_HEADS + h
    fk = lambda ci, hk: ci * GDN_QK_HEADS + hk
    bf = lambda y: y.astype(BF16)

    @pl.when(step == 0)
    def _():
        xx_scr[:, 0:8, :] = jnp.zeros((xx_scr.shape[0], 8, LANES), F32)
        state_scr[...] = jnp.zeros(state_scr.shape, F32)
        u_scr[1] = jnp.zeros(u_scr.shape[1:], u_scr.dtype)
        wq_scr[1] = jnp.zeros(wq_scr.shape[1:], wq_scr.dtype)
        kd_scr[1] = jnp.zeros(kd_scr.shape[1:], kd_scr.dtype)
        intra_scr[1] = jnp.zeros(intra_scr.shape[1:], intra_scr.dtype)
        eg_scr[1] = jnp.zeros(eg_scr.shape[1:], eg_scr.dtype)

    @pl.when(step > 0)
    def _():
        xx_scr[:, 0:8, :] = xx_scr[:, rows:rows + 8, :]

    for slab in range(xx_scr.shape[0]):
        xx_scr[slab, 8:8 + rows, :] = qkv_ref[:, slab * LANES:(slab + 1) * LANES].astype(F32)

    def row_blocks():
        return [(32 * g + r) for g in range(c // 32) for r in range(4)]

    def perm_time(i):
        return ((i >> 5) << 5) + ((i & 7) << 2) + ((i >> 3) & 3)

    def conv_silu(ci, slab):
        sl = slice(slab * LANES, (slab + 1) * LANES)
        w = [convw_ref[k:k + 1, sl] for k in range(CONV_WIDTH)]
        outs = []
        for g in range(c // 32):
            z = {rho: xx_scr[slab, pl.ds(8 + c * ci + 32 * g + rho, 8, stride=4), :]
                 for rho in range(1 - CONV_WIDTH, 4)}
            for r in range(4):
                y = w[CONV_WIDTH - 1] * z[r]
                for d in range(1, CONV_WIDTH):
                    y = y + w[CONV_WIDTH - 1 - d] * z[r - d]
                outs.append(y)
        y = jnp.concatenate(outs, axis=0)
        return y * jax.nn.sigmoid(y)

    def l2n(y):
        return y * lax.rsqrt(jnp.sum(y * y, axis=-1, keepdims=True) + EPS)

    lane = lax.broadcasted_iota(jnp.int32, (c, 2 * c), 1)
    first = lane < c
    zeros_c = jnp.zeros((c, 2 * c), BF16)

    def blockdiag(x):
        xb = bf(x)
        return jnp.concatenate([jnp.where(first, xb, zeros_c), jnp.where(first, zeros_c, xb)], axis=0)

    def recurrence():
        s = {h: state_scr[h] for h in range(GDN_V_HEADS)}
        o = {}
        for ci in cis:
            heads = range(GDN_V_HEADS)
            ws = {h: _dot(wq_scr[rslot, fh(ci, h)], bf(s[h])) for h in heads}
            yield
            vnb = {h: bf(u_scr[rslot, fh(ci, h)] - ws[h][:c]) for h in heads}
            zv = jnp.zeros((c, GDN_DIM), BF16)
            for hk in range(GDN_QK_HEADS):
                a_, b_ = rep * hk, rep * hk + 1
                vbd = jnp.concatenate([jnp.concatenate([vnb[a_], zv], axis=1),
                                       jnp.concatenate([zv, vnb[b_]], axis=1)], axis=0)
                ov = _dot(intra_scr[rslot, fk(ci, hk)], vbd)
                o[ci, a_] = ws[a_][c:] + ov[:, :GDN_DIM]
                o[ci, b_] = ws[b_][c:] + ov[:, GDN_DIM:]
            s = {h: s[h] * eg_scr[rslot, fh(ci, h)][0:1, :] + lax.dot_general(
                kd_scr[rslot, fh(ci, h)], vnb[h], TN_DIMS, preferred_element_type=F32) for h in heads}
            yield
        for h in range(GDN_V_HEADS):
            state_scr[h] = s[h]
        for ci, h in hs:
            y = _rms(o[ci, h], onorm_ref[...])
            for m, row0 in enumerate(row_blocks()):
                out_scr[fh(ci, h), pl.ds(row0, 8, stride=4), :] = y[8 * m:8 * m + 8]
        for ci, h in hs:
            sl = slice(h * GDN_DIM, (h + 1) * GDN_DIM)
            z = z_ref[c * ci:c * (ci + 1), sl].astype(F32)
            o_ref[c * ci:c * (ci + 1), sl] = (out_scr[fh(ci, h)] * (z * jax.nn.sigmoid(z))
                                              ).astype(o_ref.dtype)

    def prep():
        ii = perm_time(lax.broadcasted_iota(jnp.int32, (c, c), 0))
        jj = perm_time(lax.broadcasted_iota(jnp.int32, (c, c), 1))
        row2 = perm_time(lax.broadcasted_iota(jnp.int32, (c, 2 * c), 0))
        col2 = perm_time(jnp.where(first, lane, lane - c))
        strict = row2 > col2
        causal = row2 >= col2
        hp = lax.Precision.HIGHEST
        dup = lambda m: jnp.concatenate([m, m], axis=1)
        tril = jnp.where(ii >= jj, 1.0, 0.0).astype(F32)
        triu2 = dup(jnp.where(ii <= jj, 1.0, 0.0).astype(F32))
        eye2 = dup(jnp.where(ii == jj, 1.0, 0.0).astype(F32))
        gc_row, beta_row, gcol, egc, kdf = {}, {}, {}, {}, {}
        for ci in cis:
            ba = jnp.concatenate([ba_ref[pl.ds(c * ci + row0, 8, stride=4), :]
                                  for row0 in row_blocks()], axis=0)
            beta_col = jax.nn.sigmoid(ba)
            g_col = -jnp.exp(prow_ref[0:1, :]) * _softplus(ba + prow_ref[1:2, :])
            gc_col = jnp.dot(tril, g_col, precision=hp, preferred_element_type=F32)
            gc_row[ci] = lax.dot_general(g_col, triu2, TN_DIMS, precision=hp,
                                         preferred_element_type=F32)
            beta_row[ci] = lax.dot_general(beta_col, eye2, TN_DIMS, precision=hp,
                                           preferred_element_type=F32)
            g_last = gc_col[c - 1:c, :]
            egc_all = jnp.exp(gc_col)
            kdf_all = pltpu.roll(beta_col, GDN_V_HEADS, axis=1) * jnp.exp(g_last - gc_col)
            eg_all = jnp.exp(g_last)
            for h in range(GDN_V_HEADS):
                ln = slice(GDN_V_HEADS + h, GDN_V_HEADS + h + 1)
                gcol[ci, h] = jnp.broadcast_to(gc_col[:, ln], (c, 2 * c))
                egc[ci, h] = jnp.broadcast_to(egc_all[:, ln], (c, GDN_DIM))
                kdf[ci, h] = jnp.broadcast_to(kdf_all[:, ln], (c, GDN_DIM))
                eg_scr[wslot, fh(ci, h)] = jnp.broadcast_to(eg_all[:, ln], eg_scr.shape[2:])
        q = {(ci, hk): l2n(conv_silu(ci, hk)) * (GDN_DIM ** -0.5) for ci, hk in hks}
        k = {(ci, hk): l2n(conv_silu(ci, GDN_QK_HEADS + hk)) for ci, hk in hks}
        gram = {x: lax.dot_general(jnp.concatenate([bf(k[x]), bf(q[x])], axis=0),
                                   jnp.concatenate([bf(k[x]), bf(k[x])], axis=0),
                                   NT_DIMS, preferred_element_type=F32) for x in hks}
        yield
        pick = lambda m, hk: jnp.where(first[0:1, :], m[rep * hk:rep * hk + 1, :],
                                       m[rep * hk + 1:rep * hk + 2, :])
        a = {}
        for ci, hk in hks:
            gcol2 = jnp.where(first, gcol[ci, rep * hk], gcol[ci, rep * hk + 1])
            dec = jnp.where(causal, jnp.exp(gcol2 - pick(gc_row[ci][GDN_V_HEADS:], hk)), 0.0)
            brow = pick(beta_row[ci], hk)
            a[ci, hk] = jnp.where(strict, gram[ci, hk][:c] * dec, 0.0) * brow
            intra_scr[wslot, fk(ci, hk)] = bf(gram[ci, hk][c:] * dec * brow)
        n = {x: -a[x] for x in hks}
        p = {x: _dot(bf(a[x]), blockdiag(a[x])) for x in hks}
        yield
        for _ in range(4):
            r_ = {x: _dot(jnp.concatenate([bf(n[x]), bf(p[x])], axis=0), blockdiag(p[x])) for x in hks}
            yield
            n = {x: n[x] + p[x] + r_[x][:c] for x in hks}
            p = {x: r_[x][c:] for x in hks}
        n = {x: n[x] + p[x] + _dot(bf(n[x]), blockdiag(p[x])) for x in hks}
        yield
        rhs = {(ci, h): jnp.concatenate([conv_silu(ci, 2 * GDN_QK_HEADS + h),
                                         k[ci, h // rep] * egc[ci, h]], axis=1) for ci, h in hs}
        zr = jnp.zeros((c, 2 * GDN_DIM), BF16)
        for ci, hk in hks:
            a_, b_ = rep * hk, rep * hk + 1
            rbd = jnp.concatenate([jnp.concatenate([bf(rhs[ci, a_]), zr], axis=1),
                                   jnp.concatenate([zr, bf(rhs[ci, b_])], axis=1)], axis=0)
            uw = (jnp.concatenate([rhs[ci, a_], rhs[ci, b_]], axis=1)
                  + _dot(bf(n[ci, hk]), rbd))
            for i_, h in enumerate((a_, b_)):
                o0 = 2 * GDN_DIM * i_
                u_scr[wslot, fh(ci, h)] = uw[:, o0:o0 + GDN_DIM]
                wq_scr[wslot, fh(ci, h)] = jnp.concatenate(
                    [bf(uw[:, o0 + GDN_DIM:o0 + 2 * GDN_DIM]), bf(q[ci, hk] * egc[ci, h])], axis=0)
                kd_scr[wslot, fh(ci, h)] = bf(k[ci, hk] * kdf[ci, h])

    _interleave([recurrence(), prep()])


def _gdn(proj3d, ba3d, conv_w, prow, onorm, *, qkv_col, z_col):
    b, t, _ = proj3d.shape
    c = CHUNK
    rows = c * GDN_CPB
    nt = t // rows
    conv_ch = conv_w.shape[1]
    v_width = GDN_V_HEADS * GDN_DIM
    cur = lambda bi, si: (bi, jnp.minimum(si, nt - 1))
    prev = lambda bi, si: (bi, jnp.maximum(si - 1, 0))
    const = lambda bi, si: (0, 0)
    heads = (2, GDN_CPB * GDN_V_HEADS)
    return pl.pallas_call(
        _gdn_body,
        grid=(b, nt + 1),
        in_specs=[
            pl.BlockSpec((None, rows, conv_ch), lambda bi, si: cur(bi, si) + (qkv_col,)),
            pl.BlockSpec((None, rows, v_width), lambda bi, si: prev(bi, si) + (z_col,)),
            pl.BlockSpec((None, rows, LANES), lambda bi, si: cur(bi, si) + (0,)),
            pl.BlockSpec((CONV_WIDTH, conv_ch), const),
            pl.BlockSpec((8, LANES), const),
            pl.BlockSpec((1, GDN_DIM), const),
        ],
        out_specs=pl.BlockSpec((None, rows, v_width), lambda bi, si: prev(bi, si) + (0,)),
        out_shape=jax.ShapeDtypeStruct((b, t, v_width), BF16),
        scratch_shapes=[pltpu.VMEM((conv_ch // LANES, rows + 8, LANES), F32),
                        pltpu.VMEM((GDN_V_HEADS, GDN_DIM, GDN_DIM), F32),
                        pltpu.VMEM(heads + (c, GDN_DIM), F32),
                        pltpu.VMEM(heads + (2 * c, GDN_DIM), BF16),
                        pltpu.VMEM(heads + (c, GDN_DIM), BF16),
                        pltpu.VMEM((2, GDN_CPB * GDN_QK_HEADS, c, 2 * c), BF16),
                        pltpu.VMEM(heads + (8, LANES), F32),
                        pltpu.VMEM(heads[1:] + (c, GDN_DIM), F32)],
        compiler_params=pltpu.CompilerParams(
            dimension_semantics=("parallel", "arbitrary"), vmem_limit_bytes=VMEM_LIMIT),
        name="gdn",
    )(proj3d, proj3d, ba3d, conv_w, prow, onorm)


def _post_body(x_ref, oa_ref, ob_ref, ga_ref, gb_ref, p_ref, wa_ref, wb_ref, wo_ref, g_mix_ref,
               g_pre_ref, wup_ref, wdn_ref, g_post_ref, wple_ref, wgate_ref, g_ple_ref, out_ref,
               *, ff_chunk):
    ya = _dot(oa_ref[...], wa_ref[...])
    yb = _dot(ob_ref[...], wb_ref[...])
    merged = (jax.nn.sigmoid(ga_ref[...].astype(F32)) * ya
              + jax.nn.sigmoid(gb_ref[...].astype(F32)) * yb)
    h = x_ref[...] + _rms(_dot(merged.astype(BF16), wo_ref[...]), g_mix_ref[...])
    u = _rms(h, g_pre_ref[...]).astype(BF16)
    d_ff = wup_ref.shape[1]
    acc = jnp.zeros(h.shape, F32)
    for c0 in range(0, d_ff, ff_chunk):
        hid = jnp.square(jnp.maximum(_dot(u, wup_ref[:, c0:c0 + ff_chunk]), 0.0))
        acc = acc + _dot(hid.astype(BF16), wdn_ref[c0:c0 + ff_chunk, :])
    h = h + _rms(acc, g_post_ref[...])
    e = _dot(p_ref[...].astype(BF16), wple_ref[...]) * jax.nn.sigmoid(
        _dot(h.astype(BF16), wgate_ref[...]))
    out_ref[...] = h + _rms(e, g_ple_ref[...])


def _post(x2d, o_a, o_b, proj, p2d, w_a, w_b, w_o, g_mix, g_pre, w_up, w_dn, g_post, w_ple, w_gate,
          g_ple, *, ga_col, gb_col, tm=512, ff_chunk=1024):
    n, d = x2d.shape
    row = lambda i: (i, 0)
    const = lambda i: (0, 0)
    weight = lambda w: pl.BlockSpec(w.shape, const, pipeline_mode=pl.Buffered(1))
    gain = pl.BlockSpec((1, d), const)
    return pl.pallas_call(
        functools.partial(_post_body, ff_chunk=ff_chunk),
        grid=(n // tm,),
        in_specs=[
            pl.BlockSpec((tm, d), row),
            pl.BlockSpec((tm, o_a.shape[1]), row),
            pl.BlockSpec((tm, o_b.shape[1]), row),
            pl.BlockSpec((tm, d), lambda i: (i, ga_col)),
            pl.BlockSpec((tm, d), lambda i: (i, gb_col)),
            pl.BlockSpec((tm, p2d.shape[1]), row),
            weight(w_a), weight(w_b), weight(w_o), gain,
            gain, weight(w_up), weight(w_dn), gain,
            weight(w_ple), weight(w_gate), gain,
        ],
        out_specs=pl.BlockSpec((tm, d), row),
        out_shape=jax.ShapeDtypeStruct((n, d), F32),
        compiler_params=pltpu.CompilerParams(
            dimension_semantics=("parallel",), vmem_limit_bytes=VMEM_LIMIT),
        name="post",
    )(x2d, o_a, o_b, proj, proj, p2d, w_a, w_b, w_o, g_mix, g_pre, w_up, w_dn, g_post, w_ple,
      w_gate, g_ple)


def _layer(h, p_i, i, pre_mix_norm, w_in, conv_w, lambda_q1, lambda_k1, lambda_q2, lambda_k2,
           da_sub_norm, gdn_a_log, gdn_dt_bias, gdn_out_norm, w_branch_a, w_branch_b, w_out,
           post_mix_norm, pre_mlp_norm, w_up, w_down, post_mlp_norm, w_ple, w_ple_gate, ple_norm):
    b, t, d = h.shape
    n = b * t
    da_w = DA_HEADS * 2 * DA_HEAD_DIM
    gqk_w = GDN_QK_HEADS * GDN_DIM
    gv_w = GDN_V_HEADS * GDN_DIM
    conv_ch = 2 * gqk_w + gv_w
    sizes = (da_w, da_w, da_w, conv_ch, gv_w, GDN_V_HEADS, GDN_V_HEADS, d, d)
    offs = [0]
    for s in sizes:
        offs.append(offs[-1] + s)
    w_t = w_in.T
    piece = lambda j: w_t[offs[j]:offs[j + 1]]
    w_main_t = jnp.concatenate([piece(3), piece(4), piece(0) * math.log2(math.e), piece(1), piece(2),
                                piece(7), piece(8)], axis=0).astype(BF16)
    w_ba_t = jnp.pad(jnp.concatenate([piece(5), piece(6)], axis=0),
                     ((0, LANES - 2 * GDN_V_HEADS), (0, 0))).astype(BF16)
    col = {"conv": 0, "z": conv_ch, "q": conv_ch + gv_w, "k": conv_ch + gv_w + da_w,
           "v": conv_ch + gv_w + 2 * da_w, "ga": conv_ch + gv_w + 3 * da_w,
           "gb": conv_ch + gv_w + 3 * da_w + d}

    x2d = h.reshape(n, d)
    proj, ba = _in_proj(x2d, pre_mix_norm.reshape(1, d), w_main_t, w_ba_t)
    proj3d = proj.reshape(b, t, -1)

    lambda_init = 0.8 - 0.6 * math.exp(-0.3 * i)
    o_a = _diff_attn(proj3d, lambda_q1.reshape(1, -1), lambda_k1.reshape(1, -1),
                     lambda_q2.reshape(1, -1), lambda_k2.reshape(1, -1), da_sub_norm.reshape(1, -1),
                     q_col=col["q"] // LANES, k_col=col["k"] // LANES, v_col=col["v"] // LANES,
                     lambda_init=lambda_init)

    lanes = jnp.zeros((LANES,), F32)
    a_log = lanes.at[GDN_V_HEADS:2 * GDN_V_HEADS].set(gdn_a_log.astype(F32))
    dt_b = lanes.at[GDN_V_HEADS:2 * GDN_V_HEADS].set(gdn_dt_bias.astype(F32))
    prow = jnp.zeros((8, LANES), F32).at[0].set(a_log).at[1].set(dt_b)
    o_b = _gdn(proj3d, ba.reshape(b, t, LANES), conv_w, prow, gdn_out_norm.reshape(1, -1),
               qkv_col=col["conv"] // conv_ch, z_col=col["z"] // gv_w)

    out = _post(x2d, o_a.reshape(n, -1), o_b.reshape(n, -1), proj, p_i.reshape(n, -1),
                w_branch_a.astype(BF16), w_branch_b.astype(BF16), w_out.astype(BF16),
                post_mix_norm.reshape(1, d), pre_mlp_norm.reshape(1, d), w_up.astype(BF16),
                w_down.astype(BF16), post_mlp_norm.reshape(1, d), w_ple.astype(BF16),
                w_ple_gate.astype(BF16), ple_norm.reshape(1, d),
                ga_col=col["ga"] // d, gb_col=col["gb"] // d)
    return out.reshape(b, t, d)


def kernel(x, p, pre_mix_norm, w_in, conv_w, lambda_q1, lambda_k1, lambda_q2, lambda_k2,
           da_sub_norm, gdn_a_log, gdn_dt_bias, gdn_out_norm, w_branch_a, w_branch_b, w_out,
           post_mix_norm, pre_mlp_norm, w_up, w_down, post_mlp_norm, w_ple, w_ple_gate, ple_norm):
    per_layer = (pre_mix_norm, w_in, conv_w, lambda_q1, lambda_k1, lambda_q2, lambda_k2,
                 da_sub_norm, gdn_a_log, gdn_dt_bias, gdn_out_norm, w_branch_a, w_branch_b, w_out,
                 post_mix_norm, pre_mlp_norm, w_up, w_down, post_mlp_norm, w_ple, w_ple_gate,
                 ple_norm)
    h = x
    for i in range(p.shape[0]):
        h = _layer(h, p[i], i, *(w[i] for w in per_layer))
    return h
```

```python
import functools
import math

import jax
import jax.numpy as jnp
from jax import lax
from jax.experimental import pallas as pl
from jax.experimental.pallas import tpu as pltpu

F32 = jnp.float32
BF16 = jnp.bfloat16

EPS = 1e-6
LANES = 128
DA_HEADS = 8
DA_HEAD_DIM = 64
GDN_QK_HEADS = 8
GDN_V_HEADS = 16
GDN_DIM = 128
CONV_WIDTH = 4
CHUNK = 64
GDN_CPB = 2
GDN_IN_BLOCK = 1024
ATTN_CHAINS = 4
NEG = -1e30
VMEM_LIMIT = 56 * 1024 * 1024

NT_DIMS = (((1,), (1,)), ((), ()))
TN_DIMS = (((0,), (0,)), ((), ()))


def _rms(x, gain):
    return x * lax.rsqrt(jnp.mean(x * x, axis=-1, keepdims=True) + EPS) * gain


def _dot(a, b):
    return jnp.dot(a, b, preferred_element_type=F32)


def _inproj_body(x_ref, gain_ref, w_ref, wg_ref, wba_ref, out_ref, ba_ref, u_scr, *, n_head_blocks):
    j = pl.program_id(1)

    @pl.when(j == 0)
    def _():
        u = _rms(x_ref[...], gain_ref[...]).astype(BF16)
        u_scr[...] = u
        ba_ref[...] = lax.dot_general(u, wba_ref[...], NT_DIMS, preferred_element_type=F32)

    def project(w):
        out_ref[...] = lax.dot_general(u_scr[...], w[...], NT_DIMS,
                                       preferred_element_type=F32).astype(out_ref.dtype)

    pl.when(j < n_head_blocks)(lambda: project(w_ref))
    pl.when(j >= n_head_blocks)(lambda: project(wg_ref))


def _in_proj(x2d, gain, w_t, w_gate_t, w_ba_t, *, n_head, tm=2048, tn=1024):
    n, d = x2d.shape
    n_head_blocks = n_head // tn
    width = n_head + w_gate_t.shape[0]
    return pl.pallas_call(
        functools.partial(_inproj_body, n_head_blocks=n_head_blocks),
        grid=(n // tm, width // tn),
        in_specs=[
            pl.BlockSpec((tm, d), lambda i, j: (i, 0)),
            pl.BlockSpec((1, d), lambda i, j: (0, 0)),
            pl.BlockSpec((tn, d), lambda i, j: (jnp.minimum(j, n_head_blocks - 1), 0)),
            pl.BlockSpec((tn, d), lambda i, j: (jnp.maximum(j - n_head_blocks, 0), 0)),
            pl.BlockSpec((LANES, d), lambda i, j: (0, 0)),
        ],
        out_specs=[
            pl.BlockSpec((tm, tn), lambda i, j: (i, j)),
            pl.BlockSpec((tm, LANES), lambda i, j: (i, 0)),
        ],
        out_shape=[
            jax.ShapeDtypeStruct((n, width), BF16),
            jax.ShapeDtypeStruct((n, LANES), F32),
        ],
        scratch_shapes=[pltpu.VMEM((tm, d), BF16)],
        compiler_params=pltpu.CompilerParams(
            dimension_semantics=("parallel", "arbitrary"), vmem_limit_bytes=VMEM_LIMIT),
        name="in_proj",
    )(x2d, gain, w_t, w_gate_t, w_ba_t)


def _interleave(chains):
    chains = list(chains)
    while chains:
        for g in list(chains):
            if next(g, StopIteration) is StopIteration:
                chains.remove(g)


def _attn_body(lq1_ref, lk1_ref, lq2_ref, lk2_ref, gain_ref, q_ref, k_ref, v_ref, o_ref,
               qt_scr, vt_scr, *, tq, lambda_init):
    t = q_ref.shape[0]
    nq = t // tq
    scale = jnp.asarray(DA_HEAD_DIM ** -0.5, BF16)
    for i in range(nq):
        sl = slice(i * tq, (i + 1) * tq)
        qt_scr[:, sl] = (q_ref[sl, :] * scale).T
        vt_scr[0:LANES, sl] = v_ref[sl, :].T
    vt_scr[LANES:, :] = jnp.ones((vt_scr.shape[0] - LANES, t), BF16)
    lam = (jnp.exp(jnp.sum(lq1_ref[...] * lk1_ref[...], axis=-1, keepdims=True))
           - jnp.exp(jnp.sum(lq2_ref[...] * lk2_ref[...], axis=-1, keepdims=True)) + lambda_init)
    feat = lax.broadcasted_iota(jnp.int32, (LANES, tq), 0)
    kv_row = lax.broadcasted_iota(jnp.int32, (tq, 2 * tq), 0)
    q_col = lax.broadcasted_iota(jnp.int32, (tq, 2 * tq), 1)
    on_or_below_diag = kv_row <= jnp.where(q_col >= tq, q_col - tq, q_col)

    def chain(tiles):
        for qi in tiles:
            qsl = slice(qi * tq, (qi + 1) * tq)
            qt = qt_scr[:, qsl]
            zero = jnp.zeros_like(qt)
            qs = jnp.concatenate([jnp.where(feat < DA_HEAD_DIM, qt, zero),
                                  jnp.where(feat >= DA_HEAD_DIM, qt, zero)], axis=1)
            scores = lambda j: _dot(k_ref[j * tq:(j + 1) * tq, :], qs)
            m = jnp.full((1, 2 * tq), NEG, F32)
            acc = jnp.zeros((vt_scr.shape[0], 2 * tq), F32)
            s_next = scores(0)
            yield
            for j in range(qi + 1):
                s = s_next
                if j < qi:
                    s_next = scores(j + 1)
                    yield
                else:
                    s = jnp.where(on_or_below_diag, s, NEG)
                m_new = jnp.maximum(m, jnp.max(s, axis=0, keepdims=True))
                alpha = jnp.exp2(m - m_new)
                p = jnp.exp2(s - m_new)
                pv = _dot(vt_scr[:, j * tq:(j + 1) * tq], p.astype(BF16))
                yield
                acc = alpha * acc + pv
                m = m_new
            on = acc[0:LANES] * (1.0 / acc[LANES:LANES + 1])
            o = on[:, :tq] - lam * on[:, tq:]
            y = o * lax.rsqrt(jnp.mean(o * o, axis=0, keepdims=True) + EPS)
            o_ref[qsl, :] = (y.T * gain_ref[...] * (1.0 - lambda_init)).astype(o_ref.dtype)

    pairs = [[nq - 1 - i, i] for i in range(nq // 2)]
    for c0 in range(0, len(pairs), ATTN_CHAINS):
        _interleave(chain(tiles) for tiles in pairs[c0:c0 + ATTN_CHAINS])


def _diff_attn(proj3d, lq1, lk1, lq2, lk2, sub_gain, *, q_col, k_col, v_col, lambda_init, tq=256):
    b, t, _ = proj3d.shape
    vec = pl.BlockSpec((1, DA_HEAD_DIM), lambda bi, h: (0, 0))
    head = lambda col: pl.BlockSpec((None, t, LANES), lambda bi, h: (bi, 0, col + h))
    return pl.pallas_call(
        functools.partial(_attn_body, tq=tq, lambda_init=lambda_init),
        grid=(b, DA_HEADS),
        in_specs=[vec, vec, vec, vec, pl.BlockSpec((1, LANES), lambda bi, h: (0, 0)),
                  head(q_col), head(k_col), head(v_col)],
        out_specs=pl.BlockSpec((None, t, LANES), lambda bi, h: (bi, 0, h)),
        out_shape=jax.ShapeDtypeStruct((b, t, DA_HEADS * LANES), BF16),
        scratch_shapes=[pltpu.VMEM((LANES, t), BF16), pltpu.VMEM((LANES + 16, t), BF16)],
        compiler_params=pltpu.CompilerParams(
            dimension_semantics=("parallel", "parallel"), vmem_limit_bytes=VMEM_LIMIT),
        name="diff_attn",
    )(lq1, lk1, lq2, lk2, sub_gain, proj3d, proj3d, proj3d)


def _softplus(x):
    return jnp.maximum(x, 0.0) + jnp.log1p(jnp.exp(-jnp.abs(x)))


def _gdn_body(*refs):
    n_qkv = (2 * GDN_QK_HEADS + GDN_V_HEADS) * GDN_DIM // GDN_IN_BLOCK
    n_z = GDN_V_HEADS * GDN_DIM // GDN_IN_BLOCK
    qkv_refs, z_refs = refs[:n_qkv], refs[n_qkv:n_qkv + n_z]
    (ba_ref, convw_ref, prow_ref, onorm_ref, o_ref,
     xx_scr, state_scr, u_scr, wq_scr, kd_scr, intra_scr, eg_scr, out_scr) = refs[n_qkv + n_z:]
    slabs_per_block = GDN_IN_BLOCK // LANES
    c = CHUNK
    rows = ba_ref.shape[0]
    cis = list(range(rows // c))
    step = pl.program_id(1)
    wslot = step % 2
    rslot = 1 - wslot
    rep = GDN_V_HEADS // GDN_QK_HEADS
    hks = [(ci, hk) for ci in cis for hk in range(GDN_QK_HEADS)]
    hs = [(ci, h) for ci in cis for h in range(GDN_V_HEADS)]
    fh = lambda ci, h: ci * GDN_V_HEADS + h
    fk = lambda ci, hk: ci * GDN_QK_HEADS + hk
    bf = lambda y: y.astype(BF16)

    @pl.when(step == 0)
    def _():
        xx_scr[:, 0:8, :] = jnp.zeros((xx_scr.shape[0], 8, LANES), F32)
        state_scr[...] = jnp.zeros(state_scr.shape, F32)
        u_scr[1] = jnp.zeros(u_scr.shape[1:], u_scr.dtype)
        wq_scr[1] = jnp.zeros(wq_scr.shape[1:], wq_scr.dtype)
        kd_scr[1] = jnp.zeros(kd_scr.shape[1:], kd_scr.dtype)
        intra_scr[1] = jnp.zeros(intra_scr.shape[1:], intra_scr.dtype)
        eg_scr[1] = jnp.zeros(eg_scr.shape[1:], eg_scr.dtype)

    @pl.when(step > 0)
    def _():
        xx_scr[:, 0:8, :] = xx_scr[:, rows:rows + 8, :]

    for slab in range(xx_scr.shape[0]):
        blk, off = divmod(slab, slabs_per_block)
        xx_scr[slab, 8:8 + rows, :] = qkv_refs[blk][:, off * LANES:(off + 1) * LANES].astype(F32)

    def row_blocks():
        return [(32 * g + r) for g in range(c // 32) for r in range(4)]

    def perm_time(i):
        return ((i >> 5) << 5) + ((i & 7) << 2) + ((i >> 3) & 3)

    def conv_silu(ci, slab):
        sl = slice(slab * LANES, (slab + 1) * LANES)
        w = [convw_ref[k:k + 1, sl] for k in range(CONV_WIDTH)]
        outs = []
        for g in range(c // 32):
            z = {rho: xx_scr[slab, pl.ds(8 + c * ci + 32 * g + rho, 8, stride=4), :]
                 for rho in range(1 - CONV_WIDTH, 4)}
            for r in range(4):
                y = w[CONV_WIDTH - 1] * z[r]
                for d in range(1, CONV_WIDTH):
                    y = y + w[CONV_WIDTH - 1 - d] * z[r - d]
                outs.append(y)
        y = jnp.concatenate(outs, axis=0)
        return y * jax.nn.sigmoid(y)

    def l2n(y):
        return y * lax.rsqrt(jnp.sum(y * y, axis=-1, keepdims=True) + EPS)

    lane = lax.broadcasted_iota(jnp.int32, (c, 2 * c), 1)
    first = lane < c
    zeros_c = jnp.zeros((c, 2 * c), BF16)

    def blockdiag(x):
        xb = bf(x)
        return jnp.concatenate([jnp.where(first, xb, zeros_c), jnp.where(first, zeros_c, xb)], axis=0)

    def recurrence():
        s = {h: state_scr[h] for h in range(GDN_V_HEADS)}
        o = {}
        for ci in cis:
            heads = range(GDN_V_HEADS)
            ws = {h: _dot(wq_scr[rslot, fh(ci, h)], bf(s[h])) for h in heads}
            yield
            vnb = {h: bf(u_scr[rslot, fh(ci, h)] - ws[h][:c]) for h in heads}
            zv = jnp.zeros((c, GDN_DIM), BF16)
            for hk in range(GDN_QK_HEADS):
                a_, b_ = rep * hk, rep * hk + 1
                vbd = jnp.concatenate([jnp.concatenate([vnb[a_], zv], axis=1),
                                       jnp.concatenate([zv, vnb[b_]], axis=1)], axis=0)
                ov = _dot(intra_scr[rslot, fk(ci, hk)], vbd)
                o[ci, a_] = ws[a_][c:] + ov[:, :GDN_DIM]
                o[ci, b_] = ws[b_][c:] + ov[:, GDN_DIM:]
            s = {h: s[h] * eg_scr[rslot, fh(ci, h)][0:1, :] + lax.dot_general(
                kd_scr[rslot, fh(ci, h)], vnb[h], TN_DIMS, preferred_element_type=F32) for h in heads}
            yield
        for h in range(GDN_V_HEADS):
            state_scr[h] = s[h]
        for ci, h in hs:
            y = _rms(o[ci, h], onorm_ref[...])
            for m, row0 in enumerate(row_blocks()):
                out_scr[fh(ci, h), pl.ds(row0, 8, stride=4), :] = y[8 * m:8 * m + 8]
        for ci, h in hs:
            blk, off = divmod(h, slabs_per_block)
            z = z_refs[blk][c * ci:c * (ci + 1), off * LANES:(off + 1) * LANES].astype(F32)
            o_ref[c * ci:c * (ci + 1), h * GDN_DIM:(h + 1) * GDN_DIM] = (
                out_scr[fh(ci, h)] * (z * jax.nn.sigmoid(z))).astype(o_ref.dtype)

    def prep():
        ii = perm_time(lax.broadcasted_iota(jnp.int32, (c, c), 0))
        jj = perm_time(lax.broadcasted_iota(jnp.int32, (c, c), 1))
        row2 = perm_time(lax.broadcasted_iota(jnp.int32, (c, 2 * c), 0))
        col2 = perm_time(jnp.where(first, lane, lane - c))
        strict = row2 > col2
        causal = row2 >= col2
        hp = lax.Precision.HIGHEST
        dup = lambda m: jnp.concatenate([m, m], axis=1)
        tril = jnp.where(ii >= jj, 1.0, 0.0).astype(F32)
        triu2 = dup(jnp.where(ii <= jj, 1.0, 0.0).astype(F32))
        eye2 = dup(jnp.where(ii == jj, 1.0, 0.0).astype(F32))
        gc_row, beta_row, gcol, egc, kdf = {}, {}, {}, {}, {}
        for ci in cis:
            ba = jnp.concatenate([ba_ref[pl.ds(c * ci + row0, 8, stride=4), :]
                                  for row0 in row_blocks()], axis=0)
            beta_col = jax.nn.sigmoid(ba)
            g_col = -jnp.exp(prow_ref[0:1, :]) * _softplus(ba + prow_ref[1:2, :])
            gc_col = jnp.dot(tril, g_col, precision=hp, preferred_element_type=F32)
            gc_row[ci] = lax.dot_general(g_col, triu2, TN_DIMS, precision=hp,
                                         preferred_element_type=F32)
            beta_row[ci] = lax.dot_general(beta_col, eye2, TN_DIMS, precision=hp,
                                           preferred_element_type=F32)
            g_last = gc_col[c - 1:c, :]
            egc_all = jnp.exp(gc_col)
            kdf_all = pltpu.roll(beta_col, GDN_V_HEADS, axis=1) * jnp.exp(g_last - gc_col)
            eg_all = jnp.exp(g_last)
            for h in range(GDN_V_HEADS):
                ln = slice(GDN_V_HEADS + h, GDN_V_HEADS + h + 1)
                gcol[ci, h] = jnp.broadcast_to(gc_col[:, ln], (c, 2 * c))
                egc[ci, h] = jnp.broadcast_to(egc_all[:, ln], (c, GDN_DIM))
                kdf[ci, h] = jnp.broadcast_to(kdf_all[:, ln], (c, GDN_DIM))
                eg_scr[wslot, fh(ci, h)] = jnp.broadcast_to(eg_all[:, ln], eg_scr.shape[2:])
        q = {(ci, hk): l2n(conv_silu(ci, hk)) * (GDN_DIM ** -0.5) for ci, hk in hks}
        k = {(ci, hk): l2n(conv_silu(ci, GDN_QK_HEADS + hk)) for ci, hk in hks}
        gram = {x: lax.dot_general(jnp.concatenate([bf(k[x]), bf(q[x])], axis=0),
                                   jnp.concatenate([bf(k[x]), bf(k[x])], axis=0),
                                   NT_DIMS, preferred_element_type=F32) for x in hks}
        yield
        pick = lambda m, hk: jnp.where(first[0:1, :], m[rep * hk:rep * hk + 1, :],
                                       m[rep * hk + 1:rep * hk + 2, :])
        a = {}
        for ci, hk in hks:
            gcol2 = jnp.where(first, gcol[ci, rep * hk], gcol[ci, rep * hk + 1])
            dec = jnp.where(causal, jnp.exp(gcol2 - pick(gc_row[ci][GDN_V_HEADS:], hk)), 0.0)
            brow = pick(beta_row[ci], hk)
            a[ci, hk] = jnp.where(strict, gram[ci, hk][:c] * dec, 0.0) * brow
            intra_scr[wslot, fk(ci, hk)] = bf(gram[ci, hk][c:] * dec * brow)
        n = {x: -a[x] for x in hks}
        p = {x: _dot(bf(a[x]), blockdiag(a[x])) for x in hks}
        yield
        for _ in range(4):
            r_ = {x: _dot(jnp.concatenate([bf(n[x]), bf(p[x])], axis=0), blockdiag(p[x])) for x in hks}
            yield
            n = {x: n[x] + p[x] + r_[x][:c] for x in hks}
            p = {x: r_[x][c:] for x in hks}
        n = {x: n[x] + p[x] + _dot(bf(n[x]), blockdiag(p[x])) for x in hks}
        yield
        rhs = {(ci, h): jnp.concatenate([conv_silu(ci, 2 * GDN_QK_HEADS + h),
                                         k[ci, h // rep] * egc[ci, h]], axis=1) for ci, h in hs}
        zr = jnp.zeros((c, 2 * GDN_DIM), BF16)
        for ci, hk in hks:
            a_, b_ = rep * hk, rep * hk + 1
            rbd = jnp.concatenate([jnp.concatenate([bf(rhs[ci, a_]), zr], axis=1),
                                   jnp.concatenate([zr, bf(rhs[ci, b_])], axis=1)], axis=0)
            uw = (jnp.concatenate([rhs[ci, a_], rhs[ci, b_]], axis=1)
                  + _dot(bf(n[ci, hk]), rbd))
            for i_, h in enumerate((a_, b_)):
                o0 = 2 * GDN_DIM * i_
                u_scr[wslot, fh(ci, h)] = uw[:, o0:o0 + GDN_DIM]
                wq_scr[wslot, fh(ci, h)] = jnp.concatenate(
                    [bf(uw[:, o0 + GDN_DIM:o0 + 2 * GDN_DIM]), bf(q[ci, hk] * egc[ci, h])], axis=0)
                kd_scr[wslot, fh(ci, h)] = bf(k[ci, hk] * kdf[ci, h])

    _interleave([recurrence(), prep()])


def _gdn(proj3d, ba3d, conv_w, prow, onorm, *, qkv_col, z_col):
    b, t, _ = proj3d.shape
    c = CHUNK
    rows = c * GDN_CPB
    nt = t // rows
    conv_ch = conv_w.shape[1]
    v_width = GDN_V_HEADS * GDN_DIM
    cur = lambda bi, si: (bi, jnp.minimum(si, nt - 1))
    prev = lambda bi, si: (bi, jnp.maximum(si - 1, 0))
    const = lambda bi, si: (0, 0)
    heads = (2, GDN_CPB * GDN_V_HEADS)
    return pl.pallas_call(
        _gdn_body,
        grid=(b, nt + 1),
        in_specs=[
            *[pl.BlockSpec((None, rows, GDN_IN_BLOCK), lambda bi, si, j=j: cur(bi, si) + (qkv_col + j,))
              for j in range(conv_ch // GDN_IN_BLOCK)],
            *[pl.BlockSpec((None, rows, GDN_IN_BLOCK), lambda bi, si, j=j: prev(bi, si) + (z_col + j,))
              for j in range(v_width // GDN_IN_BLOCK)],
            pl.BlockSpec((None, rows, LANES), lambda bi, si: cur(bi, si) + (0,)),
            pl.BlockSpec((CONV_WIDTH, conv_ch), const),
            pl.BlockSpec((8, LANES), const),
            pl.BlockSpec((1, GDN_DIM), const),
        ],
        out_specs=pl.BlockSpec((None, rows, v_width), lambda bi, si: prev(bi, si) + (0,)),
        out_shape=jax.ShapeDtypeStruct((b, t, v_width), BF16),
        scratch_shapes=[pltpu.VMEM((conv_ch // LANES, rows + 8, LANES), F32),
                        pltpu.VMEM((GDN_V_HEADS, GDN_DIM, GDN_DIM), F32),
                        pltpu.VMEM(heads + (c, GDN_DIM), F32),
                        pltpu.VMEM(heads + (2 * c, GDN_DIM), BF16),
                        pltpu.VMEM(heads + (c, GDN_DIM), BF16),
                        pltpu.VMEM((2, GDN_CPB * GDN_QK_HEADS, c, 2 * c), BF16),
                        pltpu.VMEM(heads + (8, LANES), F32),
                        pltpu.VMEM(heads[1:] + (c, GDN_DIM), F32)],
        compiler_params=pltpu.CompilerParams(
            dimension_semantics=("parallel", "arbitrary"), vmem_limit_bytes=VMEM_LIMIT),
        name="gdn",
    )(*([proj3d] * ((conv_ch + v_width) // GDN_IN_BLOCK)), ba3d, conv_w, prow, onorm)


def _post_body(x_ref, oa_ref, ob_ref, ga_ref, gb_ref, p_ref, wa_ref, wb_ref, wo_ref, g_mix_ref,
               g_pre_ref, wup_ref, wdn_ref, g_post_ref, wple_ref, wgate_ref, g_ple_ref, out_ref,
               *, ff_chunk):
    ya = _dot(oa_ref[...], wa_ref[...])
    yb = _dot(ob_ref[...], wb_ref[...])
    merged = (jax.nn.sigmoid(ga_ref[...].astype(F32)) * ya
              + jax.nn.sigmoid(gb_ref[...].astype(F32)) * yb)
    h = x_ref[...] + _rms(_dot(merged.astype(BF16), wo_ref[...]), g_mix_ref[...])
    u = _rms(h, g_pre_ref[...]).astype(BF16)
    d_ff = wup_ref.shape[1]
    acc = jnp.zeros(h.shape, F32)
    for c0 in range(0, d_ff, ff_chunk):
        hid = jnp.square(jnp.maximum(_dot(u, wup_ref[:, c0:c0 + ff_chunk]), 0.0))
        acc = acc + _dot(hid.astype(BF16), wdn_ref[c0:c0 + ff_chunk, :])
    h = h + _rms(acc, g_post_ref[...])
    e = _dot(p_ref[...].astype(BF16), wple_ref[...]) * jax.nn.sigmoid(
        _dot(h.astype(BF16), wgate_ref[...]))
    out_ref[...] = h + _rms(e, g_ple_ref[...])


def _post(x2d, o_a, o_b, proj, p2d, w_a, w_b, w_o, g_mix, g_pre, w_up, w_dn, g_post, w_ple, w_gate,
          g_ple, *, ga_col, gb_col, tm=512, ff_chunk=1024):
    n, d = x2d.shape
    row = lambda i: (i, 0)
    const = lambda i: (0, 0)
    weight = lambda w: pl.BlockSpec(w.shape, const, pipeline_mode=pl.Buffered(1))
    gain = pl.BlockSpec((1, d), const)
    return pl.pallas_call(
        functools.partial(_post_body, ff_chunk=ff_chunk),
        grid=(n // tm,),
        in_specs=[
            pl.BlockSpec((tm, d), row),
            pl.BlockSpec((tm, o_a.shape[1]), row),
            pl.BlockSpec((tm, o_b.shape[1]), row),
            pl.BlockSpec((tm, d), lambda i: (i, ga_col)),
            pl.BlockSpec((tm, d), lambda i: (i, gb_col)),
            pl.BlockSpec((tm, p2d.shape[1]), row),
            weight(w_a), weight(w_b), weight(w_o), gain,
            gain, weight(w_up), weight(w_dn), gain,
            weight(w_ple), weight(w_gate), gain,
        ],
        out_specs=pl.BlockSpec((tm, d), row),
        out_shape=jax.ShapeDtypeStruct((n, d), F32),
        compiler_params=pltpu.CompilerParams(
            dimension_semantics=("parallel",), vmem_limit_bytes=VMEM_LIMIT),
        name="post",
    )(x2d, o_a, o_b, proj, proj, p2d, w_a, w_b, w_o, g_mix, g_pre, w_up, w_dn, g_post, w_ple,
      w_gate, g_ple)


def _layer(h, p_i, i, pre_mix_norm, w_in, conv_w, lambda_q1, lambda_k1, lambda_q2, lambda_k2,
           da_sub_norm, gdn_a_log, gdn_dt_bias, gdn_out_norm, w_branch_a, w_branch_b, w_out,
           post_mix_norm, pre_mlp_norm, w_up, w_down, post_mlp_norm, w_ple, w_ple_gate, ple_norm):
    b, t, d = h.shape
    n = b * t
    da_w = DA_HEADS * 2 * DA_HEAD_DIM
    gqk_w = GDN_QK_HEADS * GDN_DIM
    gv_w = GDN_V_HEADS * GDN_DIM
    conv_ch = 2 * gqk_w + gv_w
    sizes = (da_w, da_w, da_w, conv_ch, gv_w, GDN_V_HEADS, GDN_V_HEADS, d, d)
    offs = [0]
    for s in sizes:
        offs.append(offs[-1] + s)
    row_scale = jnp.ones((offs[-1], 1), F32).at[:da_w].set(math.log2(math.e))
    w_t = (w_in.T * row_scale).astype(BF16)
    n_head = offs[5]
    w_gate_t = w_t[offs[7]:]
    w_ba_t = jnp.pad(w_t[offs[5]:offs[7]], ((0, LANES - 2 * GDN_V_HEADS), (0, 0)))
    col = {"q": 0, "k": da_w, "v": 2 * da_w, "conv": 3 * da_w, "z": 3 * da_w + conv_ch,
           "ga": n_head, "gb": n_head + d}

    x2d = h.reshape(n, d)
    proj, ba = _in_proj(x2d, pre_mix_norm.reshape(1, d), w_t, w_gate_t, w_ba_t, n_head=n_head)
    proj3d = proj.reshape(b, t, -1)

    lambda_init = 0.8 - 0.6 * math.exp(-0.3 * i)
    o_a = _diff_attn(proj3d, lambda_q1.reshape(1, -1), lambda_k1.reshape(1, -1),
                     lambda_q2.reshape(1, -1), lambda_k2.reshape(1, -1), da_sub_norm.reshape(1, -1),
                     q_col=col["q"] // LANES, k_col=col["k"] // LANES, v_col=col["v"] // LANES,
                     lambda_init=lambda_init)

    lanes = jnp.zeros((LANES,), F32)
    a_log = lanes.at[GDN_V_HEADS:2 * GDN_V_HEADS].set(gdn_a_log.astype(F32))
    dt_b = lanes.at[GDN_V_HEADS:2 * GDN_V_HEADS].set(gdn_dt_bias.astype(F32))
    prow = jnp.zeros((8, LANES), F32).at[0].set(a_log).at[1].set(dt_b)
    o_b = _gdn(proj3d, ba.reshape(b, t, LANES), conv_w, prow, gdn_out_norm.reshape(1, -1),
               qkv_col=col["conv"] // GDN_IN_BLOCK, z_col=col["z"] // GDN_IN_BLOCK)

    out = _post(x2d, o_a.reshape(n, -1), o_b.reshape(n, -1), proj, p_i.reshape(n, -1),
                w_branch_a.astype(BF16), w_branch_b.astype(BF16), w_out.astype(BF16),
                post_mix_norm.reshape(1, d), pre_mlp_norm.reshape(1, d), w_up.astype(BF16),
                w_down.astype(BF16), post_mlp_norm.reshape(1, d), w_ple.astype(BF16),
                w_ple_gate.astype(BF16), ple_norm.reshape(1, d),
                ga_col=col["ga"] // d, gb_col=col["gb"] // d)
    return out.reshape(b, t, d)


def kernel(x, p, pre_mix_norm, w_in, conv_w, lambda_q1, lambda_k1, lambda_q2, lambda_k2,
           da_sub_norm, gdn_a_log, gdn_dt_bias, gdn_out_norm, w_branch_a, w_branch_b, w_out,
           post_mix_norm, pre_mlp_norm, w_up, w_down, post_mlp_norm, w_ple, w_ple_gate, ple_norm):
    per_layer = (pre_mix_norm, w_in, conv_w, lambda_q1, lambda_k1, lambda_q2, lambda_k2,
                 da_sub_norm, gdn_a_log, gdn_dt_bias, gdn_out_norm, w_branch_a, w_branch_b, w_out,
                 post_mix_norm, pre_mlp_norm, w_up, w_down, post_mlp_norm, w_ple, w_ple_gate,
                 ple_norm)
    h = x
    for i in range(p.shape[0]):
        h = _layer(h, p[i], i, *(w[i] for w in per_layer))
    return h
```

```python
import functools
import math

import jax
import jax.numpy as jnp
from jax import lax
from jax.experimental import pallas as pl
from jax.experimental.pallas import tpu as pltpu

F32 = jnp.float32
BF16 = jnp.bfloat16

EPS = 1e-6
LANES = 128
DA_HEADS = 8
DA_HEAD_DIM = 64
GDN_QK_HEADS = 8
GDN_V_HEADS = 16
GDN_DIM = 128
CONV_WIDTH = 4
CHUNK = 64
GDN_CPB = 2
GDN_IN_BLOCK = 1024
ATTN_CHAINS = 4
NEG = -1e30
VMEM_LIMIT = 56 * 1024 * 1024

NT_DIMS = (((1,), (1,)), ((), ()))
TN_DIMS = (((0,), (0,)), ((), ()))


def _rms(x, gain):
    return x * lax.rsqrt(jnp.mean(x * x, axis=-1, keepdims=True) + EPS) * gain


def _dot(a, b):
    return jnp.dot(a, b, preferred_element_type=F32)


def _inproj_body(x_ref, gain_ref, w_ref, wg_ref, wba_ref, out_ref, ba_ref, u_scr, *, n_head_blocks):
    j = pl.program_id(1)

    @pl.when(j == 0)
    def _():
        u = _rms(x_ref[...], gain_ref[...]).astype(BF16)
        u_scr[...] = u
        ba_ref[...] = lax.dot_general(u, wba_ref[...], NT_DIMS, preferred_element_type=F32)

    def project(w):
        out_ref[...] = lax.dot_general(u_scr[...], w[...], NT_DIMS,
                                       preferred_element_type=F32).astype(out_ref.dtype)

    pl.when(j < n_head_blocks)(lambda: project(w_ref))
    pl.when(j >= n_head_blocks)(lambda: project(wg_ref))


def _in_proj(x2d, gain, w_t, w_gate_t, w_ba_t, *, n_head, tm=2048, tn=1024):
    n, d = x2d.shape
    n_head_blocks = n_head // tn
    width = n_head + w_gate_t.shape[0]
    return pl.pallas_call(
        functools.partial(_inproj_body, n_head_blocks=n_head_blocks),
        grid=(n // tm, width // tn),
        in_specs=[
            pl.BlockSpec((tm, d), lambda i, j: (i, 0)),
            pl.BlockSpec((1, d), lambda i, j: (0, 0)),
            pl.BlockSpec((tn, d), lambda i, j: (jnp.minimum(j, n_head_blocks - 1), 0)),
            pl.BlockSpec((tn, d), lambda i, j: (jnp.maximum(j - n_head_blocks, 0), 0)),
            pl.BlockSpec((LANES, d), lambda i, j: (0, 0)),
        ],
        out_specs=[
            pl.BlockSpec((tm, tn), lambda i, j: (i, j)),
            pl.BlockSpec((tm, LANES), lambda i, j: (i, 0)),
        ],
        out_shape=[
            jax.ShapeDtypeStruct((n, width), BF16),
            jax.ShapeDtypeStruct((n, LANES), F32),
        ],
        scratch_shapes=[pltpu.VMEM((tm, d), BF16)],
        compiler_params=pltpu.CompilerParams(
            dimension_semantics=("parallel", "arbitrary"), vmem_limit_bytes=VMEM_LIMIT),
        name="in_proj",
    )(x2d, gain, w_t, w_gate_t, w_ba_t)


def _interleave(chains):
    chains = list(chains)
    while chains:
        for g in list(chains):
            if next(g, StopIteration) is StopIteration:
                chains.remove(g)


def _attn_body(lq1_ref, lk1_ref, lq2_ref, lk2_ref, gain_ref, q_ref, k_ref, v_ref, o_ref,
               qt_scr, vt_scr, *, tq, lambda_init):
    t = q_ref.shape[0]
    nq = t // tq
    scale = jnp.asarray(DA_HEAD_DIM ** -0.5, BF16)
    for i in range(nq):
        sl = slice(i * tq, (i + 1) * tq)
        qt_scr[:, sl] = (q_ref[sl, :] * scale).T
        vt_scr[0:LANES, sl] = v_ref[sl, :].T
    vt_scr[LANES:, :] = jnp.ones((vt_scr.shape[0] - LANES, t), BF16)
    lam = (jnp.exp(jnp.sum(lq1_ref[...] * lk1_ref[...], axis=-1, keepdims=True))
           - jnp.exp(jnp.sum(lq2_ref[...] * lk2_ref[...], axis=-1, keepdims=True)) + lambda_init)
    feat = lax.broadcasted_iota(jnp.int32, (LANES, tq), 0)
    kv_row = lax.broadcasted_iota(jnp.int32, (tq, 2 * tq), 0)
    q_col = lax.broadcasted_iota(jnp.int32, (tq, 2 * tq), 1)
    on_or_below_diag = kv_row <= jnp.where(q_col >= tq, q_col - tq, q_col)

    def chain(tiles):
        for qi in tiles:
            qsl = slice(qi * tq, (qi + 1) * tq)
            qt = qt_scr[:, qsl]
            zero = jnp.zeros_like(qt)
            qs = jnp.concatenate([jnp.where(feat < DA_HEAD_DIM, qt, zero),
                                  jnp.where(feat >= DA_HEAD_DIM, qt, zero)], axis=1)
            scores = lambda j: _dot(k_ref[j * tq:(j + 1) * tq, :], qs)
            m = jnp.full((1, 2 * tq), NEG, F32)
            acc = jnp.zeros((vt_scr.shape[0], 2 * tq), F32)
            s_next = scores(0)
            yield
            for j in range(qi + 1):
                s = s_next
                if j < qi:
                    s_next = scores(j + 1)
                    yield
                else:
                    s = jnp.where(on_or_below_diag, s, NEG)
                m_new = jnp.maximum(m, jnp.max(s, axis=0, keepdims=True))
                alpha = jnp.exp2(m - m_new)
                p = jnp.exp2(s - m_new)
                pv = _dot(vt_scr[:, j * tq:(j + 1) * tq], p.astype(BF16))
                yield
                acc = alpha * acc + pv
                m = m_new
            on = acc[0:LANES] * (1.0 / acc[LANES:LANES + 1])
            o = on[:, :tq] - lam * on[:, tq:]
            y = o * lax.rsqrt(jnp.mean(o * o, axis=0, keepdims=True) + EPS)
            o_ref[qsl, :] = (y.T * gain_ref[...] * (1.0 - lambda_init)).astype(o_ref.dtype)

    pairs = [[nq - 1 - i, i] for i in range(nq // 2)]
    for c0 in range(0, len(pairs), ATTN_CHAINS):
        _interleave(chain(tiles) for tiles in pairs[c0:c0 + ATTN_CHAINS])


def _diff_attn(proj3d, lq1, lk1, lq2, lk2, sub_gain, *, q_col, k_col, v_col, lambda_init, tq=256):
    b, t, _ = proj3d.shape
    vec = pl.BlockSpec((1, DA_HEAD_DIM), lambda bi, h: (0, 0))
    head = lambda col: pl.BlockSpec((None, t, LANES), lambda bi, h: (bi, 0, col + h))
    return pl.pallas_call(
        functools.partial(_attn_body, tq=tq, lambda_init=lambda_init),
        grid=(b, DA_HEADS),
        in_specs=[vec, vec, vec, vec, pl.BlockSpec((1, LANES), lambda bi, h: (0, 0)),
                  head(q_col), head(k_col), head(v_col)],
        out_specs=pl.BlockSpec((None, t, LANES), lambda bi, h: (bi, 0, h)),
        out_shape=jax.ShapeDtypeStruct((b, t, DA_HEADS * LANES), BF16),
        scratch_shapes=[pltpu.VMEM((LANES, t), BF16), pltpu.VMEM((LANES + 16, t), BF16)],
        compiler_params=pltpu.CompilerParams(
            dimension_semantics=("parallel", "parallel"), vmem_limit_bytes=VMEM_LIMIT),
        name="diff_attn",
    )(lq1, lk1, lq2, lk2, sub_gain, proj3d, proj3d, proj3d)


def _softplus(x):
    return jnp.maximum(x, 0.0) + jnp.log1p(jnp.exp(-jnp.abs(x)))


def _gdn_body(*refs):
    n_qkv = (2 * GDN_QK_HEADS + GDN_V_HEADS) * GDN_DIM // GDN_IN_BLOCK
    n_z = GDN_V_HEADS * GDN_DIM // GDN_IN_BLOCK
    qkv_refs, z_refs = refs[:n_qkv], refs[n_qkv:n_qkv + n_z]
    (ba_ref, convw_ref, prow_ref, onorm_ref, o_ref,
     xx_scr, state_scr, u_scr, wq_scr, kd_scr, intra_scr, eg_scr, out_scr) = refs[n_qkv + n_z:]
    slabs_per_block = GDN_IN_BLOCK // LANES
    c = CHUNK
    rows = ba_ref.shape[0]
    cis = list(range(rows // c))
    step = pl.program_id(1)
    wslot = step % 2
    rslot = 1 - wslot
    rep = GDN_V_HEADS // GDN_QK_HEADS
    hks = [(ci, hk) for ci in cis for hk in range(GDN_QK_HEADS)]
    hs = [(ci, h) for ci in cis for h in range(GDN_V_HEADS)]
    fh = lambda ci, h: ci * GDN_V_HEADS + h
    fk = lambda ci, hk: ci * GDN_QK_HEADS + hk
    bf = lambda y: y.astype(BF16)

    def load_block(first_block):
        if first_block:
            xx_scr[:, 0:8, :] = jnp.zeros((xx_scr.shape[0], 8, LANES), F32)
            state_scr[...] = jnp.zeros(state_scr.shape, F32)
        else:
            xx_scr[:, 0:8, :] = xx_scr[:, rows:rows + 8, :]
        for slab in range(xx_scr.shape[0]):
            blk, off = divmod(slab, slabs_per_block)
            xx_scr[slab, 8:8 + rows, :] = qkv_refs[blk][:, off * LANES:(off + 1) * LANES].astype(F32)

    def row_blocks():
        return [(32 * g + r) for g in range(c // 32) for r in range(4)]

    def perm_time(i):
        return ((i >> 5) << 5) + ((i & 7) << 2) + ((i >> 3) & 3)

    def conv_silu(ci, slab):
        sl = slice(slab * LANES, (slab + 1) * LANES)
        w = [convw_ref[k:k + 1, sl] for k in range(CONV_WIDTH)]
        outs = []
        for g in range(c // 32):
            z = {rho: xx_scr[slab, pl.ds(8 + c * ci + 32 * g + rho, 8, stride=4), :]
                 for rho in range(1 - CONV_WIDTH, 4)}
            for r in range(4):
                y = w[CONV_WIDTH - 1] * z[r]
                for d in range(1, CONV_WIDTH):
                    y = y + w[CONV_WIDTH - 1 - d] * z[r - d]
                outs.append(y)
        y = jnp.concatenate(outs, axis=0)
        return y * jax.nn.sigmoid(y)

    def l2n(y):
        return y * lax.rsqrt(jnp.sum(y * y, axis=-1, keepdims=True) + EPS)

    lane = lax.broadcasted_iota(jnp.int32, (c, 2 * c), 1)
    first = lane < c
    zeros_c = jnp.zeros((c, 2 * c), BF16)

    def blockdiag(x):
        xb = bf(x)
        return jnp.concatenate([jnp.where(first, xb, zeros_c), jnp.where(first, zeros_c, xb)], axis=0)

    def recurrence():
        s = {h: state_scr[h] for h in range(GDN_V_HEADS)}
        o = {}
        for ci in cis:
            heads = range(GDN_V_HEADS)
            ws = {h: _dot(wq_scr[rslot, fh(ci, h)], bf(s[h])) for h in heads}
            yield
            vnb = {h: bf(u_scr[rslot, fh(ci, h)] - ws[h][:c]) for h in heads}
            zv = jnp.zeros((c, GDN_DIM), BF16)
            for hk in range(GDN_QK_HEADS):
                a_, b_ = rep * hk, rep * hk + 1
                vbd = jnp.concatenate([jnp.concatenate([vnb[a_], zv], axis=1),
                                       jnp.concatenate([zv, vnb[b_]], axis=1)], axis=0)
                ov = _dot(intra_scr[rslot, fk(ci, hk)], vbd)
                o[ci, a_] = ws[a_][c:] + ov[:, :GDN_DIM]
                o[ci, b_] = ws[b_][c:] + ov[:, GDN_DIM:]
            s = {h: s[h] * eg_scr[rslot, fh(ci, h)][0:1, :] + lax.dot_general(
                kd_scr[rslot, fh(ci, h)], vnb[h], TN_DIMS, preferred_element_type=F32) for h in heads}
            yield
        for h in range(GDN_V_HEADS):
            state_scr[h] = s[h]
        for ci, h in hs:
            y = _rms(o[ci, h], onorm_ref[...])
            for m, row0 in enumerate(row_blocks()):
                out_scr[fh(ci, h), pl.ds(row0, 8, stride=4), :] = y[8 * m:8 * m + 8]
        for ci, h in hs:
            blk, off = divmod(h, slabs_per_block)
            z = z_refs[blk][c * ci:c * (ci + 1), off * LANES:(off + 1) * LANES].astype(F32)
            o_ref[c * ci:c * (ci + 1), h * GDN_DIM:(h + 1) * GDN_DIM] = (
                out_scr[fh(ci, h)] * (z * jax.nn.sigmoid(z))).astype(o_ref.dtype)

    def prep(cis):
        hks = [(ci, hk) for ci in cis for hk in range(GDN_QK_HEADS)]
        hs = [(ci, h) for ci in cis for h in range(GDN_V_HEADS)]
        ii = perm_time(lax.broadcasted_iota(jnp.int32, (c, c), 0))
        jj = perm_time(lax.broadcasted_iota(jnp.int32, (c, c), 1))
        row2 = perm_time(lax.broadcasted_iota(jnp.int32, (c, 2 * c), 0))
        col2 = perm_time(jnp.where(first, lane, lane - c))
        strict = row2 > col2
        causal = row2 >= col2
        hp = lax.Precision.HIGHEST
        dup = lambda m: jnp.concatenate([m, m], axis=1)
        tril = jnp.where(ii >= jj, 1.0, 0.0).astype(F32)
        triu2 = dup(jnp.where(ii <= jj, 1.0, 0.0).astype(F32))
        eye2 = dup(jnp.where(ii == jj, 1.0, 0.0).astype(F32))
        gc_row, beta_row, gcol, egc, kdf = {}, {}, {}, {}, {}
        for ci in cis:
            ba = jnp.concatenate([ba_ref[pl.ds(c * ci + row0, 8, stride=4), :]
                                  for row0 in row_blocks()], axis=0)
            beta_col = jax.nn.sigmoid(ba)
            g_col = -jnp.exp(prow_ref[0:1, :]) * _softplus(ba + prow_ref[1:2, :])
            gc_col = jnp.dot(tril, g_col, precision=hp, preferred_element_type=F32)
            gc_row[ci] = lax.dot_general(g_col, triu2, TN_DIMS, precision=hp,
                                         preferred_element_type=F32)
            beta_row[ci] = lax.dot_general(beta_col, eye2, TN_DIMS, precision=hp,
                                           preferred_element_type=F32)
            g_last = gc_col[c - 1:c, :]
            egc_all = jnp.exp(gc_col)
            kdf_all = pltpu.roll(beta_col, GDN_V_HEADS, axis=1) * jnp.exp(g_last - gc_col)
            eg_all = jnp.exp(g_last)
            for h in range(GDN_V_HEADS):
                ln = slice(GDN_V_HEADS + h, GDN_V_HEADS + h + 1)
                gcol[ci, h] = jnp.broadcast_to(gc_col[:, ln], (c, 2 * c))
                egc[ci, h] = jnp.broadcast_to(egc_all[:, ln], (c, GDN_DIM))
                kdf[ci, h] = jnp.broadcast_to(kdf_all[:, ln], (c, GDN_DIM))
                eg_scr[wslot, fh(ci, h)] = jnp.broadcast_to(eg_all[:, ln], eg_scr.shape[2:])
        q = {(ci, hk): l2n(conv_silu(ci, hk)) * (GDN_DIM ** -0.5) for ci, hk in hks}
        k = {(ci, hk): l2n(conv_silu(ci, GDN_QK_HEADS + hk)) for ci, hk in hks}
        gram = {x: lax.dot_general(jnp.concatenate([bf(k[x]), bf(q[x])], axis=0),
                                   jnp.concatenate([bf(k[x]), bf(k[x])], axis=0),
                                   NT_DIMS, preferred_element_type=F32) for x in hks}
        yield
        pick = lambda m, hk: jnp.where(first[0:1, :], m[rep * hk:rep * hk + 1, :],
                                       m[rep * hk + 1:rep * hk + 2, :])
        a = {}
        for ci, hk in hks:
            gcol2 = jnp.where(first, gcol[ci, rep * hk], gcol[ci, rep * hk + 1])
            dec = jnp.where(causal, jnp.exp(gcol2 - pick(gc_row[ci][GDN_V_HEADS:], hk)), 0.0)
            brow = pick(beta_row[ci], hk)
            a[ci, hk] = jnp.where(strict, gram[ci, hk][:c] * dec, 0.0) * brow
            intra_scr[wslot, fk(ci, hk)] = bf(gram[ci, hk][c:] * dec * brow)
        n = {x: -a[x] for x in hks}
        p = {x: _dot(bf(a[x]), blockdiag(a[x])) for x in hks}
        yield
        for _ in range(4):
            r_ = {x: _dot(jnp.concatenate([bf(n[x]), bf(p[x])], axis=0), blockdiag(p[x])) for x in hks}
            yield
            n = {x: n[x] + p[x] + r_[x][:c] for x in hks}
            p = {x: r_[x][c:] for x in hks}
        n = {x: n[x] + p[x] + _dot(bf(n[x]), blockdiag(p[x])) for x in hks}
        yield
        rhs = {(ci, h): jnp.concatenate([conv_silu(ci, 2 * GDN_QK_HEADS + h),
                                         k[ci, h // rep] * egc[ci, h]], axis=1) for ci, h in hs}
        zr = jnp.zeros((c, 2 * GDN_DIM), BF16)
        for ci, hk in hks:
            a_, b_ = rep * hk, rep * hk + 1
            rbd = jnp.concatenate([jnp.concatenate([bf(rhs[ci, a_]), zr], axis=1),
                                   jnp.concatenate([zr, bf(rhs[ci, b_])], axis=1)], axis=0)
            uw = (jnp.concatenate([rhs[ci, a_], rhs[ci, b_]], axis=1)
                  + _dot(bf(n[ci, hk]), rbd))
            for i_, h in enumerate((a_, b_)):
                o0 = 2 * GDN_DIM * i_
                u_scr[wslot, fh(ci, h)] = uw[:, o0:o0 + GDN_DIM]
                wq_scr[wslot, fh(ci, h)] = jnp.concatenate(
                    [bf(uw[:, o0 + GDN_DIM:o0 + 2 * GDN_DIM]), bf(q[ci, hk] * egc[ci, h])], axis=0)
                kd_scr[wslot, fh(ci, h)] = bf(k[ci, hk] * kdf[ci, h])

    last = pl.num_programs(1) - 1

    @pl.when(step == 0)
    def _():
        load_block(True)
        _interleave([prep([ci]) for ci in cis])

    @pl.when((step > 0) & (step < last))
    def _():
        load_block(False)
        _interleave([recurrence()] + [prep([ci]) for ci in cis])

    @pl.when(step == last)
    def _():
        _interleave([recurrence()])


def _gdn(proj3d, ba3d, conv_w, prow, onorm, *, qkv_col, z_col):
    b, t, _ = proj3d.shape
    c = CHUNK
    rows = c * GDN_CPB
    nt = t // rows
    conv_ch = conv_w.shape[1]
    v_width = GDN_V_HEADS * GDN_DIM
    cur = lambda bi, si: (bi, jnp.minimum(si, nt - 1))
    prev = lambda bi, si: (bi, jnp.maximum(si - 1, 0))
    const = lambda bi, si: (0, 0)
    heads = (2, GDN_CPB * GDN_V_HEADS)
    return pl.pallas_call(
        _gdn_body,
        grid=(b, nt + 1),
        in_specs=[
            *[pl.BlockSpec((None, rows, GDN_IN_BLOCK), lambda bi, si, j=j: cur(bi, si) + (qkv_col + j,))
              for j in range(conv_ch // GDN_IN_BLOCK)],
            *[pl.BlockSpec((None, rows, GDN_IN_BLOCK), lambda bi, si, j=j: prev(bi, si) + (z_col + j,))
              for j in range(v_width // GDN_IN_BLOCK)],
            pl.BlockSpec((None, rows, LANES), lambda bi, si: cur(bi, si) + (0,)),
            pl.BlockSpec((CONV_WIDTH, conv_ch), const),
            pl.BlockSpec((8, LANES), const),
            pl.BlockSpec((1, GDN_DIM), const),
        ],
        out_specs=pl.BlockSpec((None, rows, v_width), lambda bi, si: prev(bi, si) + (0,)),
        out_shape=jax.ShapeDtypeStruct((b, t, v_width), BF16),
        scratch_shapes=[pltpu.VMEM((conv_ch // LANES, rows + 8, LANES), F32),
                        pltpu.VMEM((GDN_V_HEADS, GDN_DIM, GDN_DIM), F32),
                        pltpu.VMEM(heads + (c, GDN_DIM), F32),
                        pltpu.VMEM(heads + (2 * c, GDN_DIM), BF16),
                        pltpu.VMEM(heads + (c, GDN_DIM), BF16),
                        pltpu.VMEM((2, GDN_CPB * GDN_QK_HEADS, c, 2 * c), BF16),
                        pltpu.VMEM(heads + (8, LANES), F32),
                        pltpu.VMEM(heads[1:] + (c, GDN_DIM), F32)],
        compiler_params=pltpu.CompilerParams(
            dimension_semantics=("parallel", "arbitrary"), vmem_limit_bytes=VMEM_LIMIT),
        name="gdn",
    )(*([proj3d] * ((conv_ch + v_width) // GDN_IN_BLOCK)), ba3d, conv_w, prow, onorm)


def _post_body(x_ref, oa_ref, ob_ref, ga_ref, gb_ref, p_ref, wa_ref, wb_ref, wo_ref, g_mix_ref,
               g_pre_ref, wup_ref, wdn_ref, g_post_ref, wple_ref, wgate_ref, g_ple_ref, out_ref,
               *, ff_chunk):
    ya = _dot(oa_ref[...], wa_ref[...])
    yb = _dot(ob_ref[...], wb_ref[...])
    merged = (jax.nn.sigmoid(ga_ref[...].astype(F32)) * ya
              + jax.nn.sigmoid(gb_ref[...].astype(F32)) * yb)
    h = x_ref[...] + _rms(_dot(merged.astype(BF16), wo_ref[...]), g_mix_ref[...])
    u = _rms(h, g_pre_ref[...]).astype(BF16)
    d_ff = wup_ref.shape[1]
    acc = jnp.zeros(h.shape, F32)
    for c0 in range(0, d_ff, ff_chunk):
        hid = jnp.square(jnp.maximum(_dot(u, wup_ref[:, c0:c0 + ff_chunk]), 0.0))
        acc = acc + _dot(hid.astype(BF16), wdn_ref[c0:c0 + ff_chunk, :])
    h = h + _rms(acc, g_post_ref[...])
    e = _dot(p_ref[...].astype(BF16), wple_ref[...]) * jax.nn.sigmoid(
        _dot(h.astype(BF16), wgate_ref[...]))
    out_ref[...] = h + _rms(e, g_ple_ref[...])


def _post(x2d, o_a, o_b, proj, p2d, w_a, w_b, w_o, g_mix, g_pre, w_up, w_dn, g_post, w_ple, w_gate,
          g_ple, *, ga_col, gb_col, tm=512, ff_chunk=1024):
    n, d = x2d.shape
    row = lambda i: (i, 0)
    const = lambda i: (0, 0)
    weight = lambda w: pl.BlockSpec(w.shape, const, pipeline_mode=pl.Buffered(1))
    gain = pl.BlockSpec((1, d), const)
    return pl.pallas_call(
        functools.partial(_post_body, ff_chunk=ff_chunk),
        grid=(n // tm,),
        in_specs=[
            pl.BlockSpec((tm, d), row),
            pl.BlockSpec((tm, o_a.shape[1]), row),
            pl.BlockSpec((tm, o_b.shape[1]), row),
            pl.BlockSpec((tm, d), lambda i: (i, ga_col)),
            pl.BlockSpec((tm, d), lambda i: (i, gb_col)),
            pl.BlockSpec((tm, p2d.shape[1]), row),
            weight(w_a), weight(w_b), weight(w_o), gain,
            gain, weight(w_up), weight(w_dn), gain,
            weight(w_ple), weight(w_gate), gain,
        ],
        out_specs=pl.BlockSpec((tm, d), row),
        out_shape=jax.ShapeDtypeStruct((n, d), F32),
        compiler_params=pltpu.CompilerParams(
            dimension_semantics=("parallel",), vmem_limit_bytes=VMEM_LIMIT),
        name="post",
    )(x2d, o_a, o_b, proj, proj, p2d, w_a, w_b, w_o, g_mix, g_pre, w_up, w_dn, g_post, w_ple,
      w_gate, g_ple)


def _layer(h, p_i, i, pre_mix_norm, w_in, conv_w, lambda_q1, lambda_k1, lambda_q2, lambda_k2,
           da_sub_norm, gdn_a_log, gdn_dt_bias, gdn_out_norm, w_branch_a, w_branch_b, w_out,
           post_mix_norm, pre_mlp_norm, w_up, w_down, post_mlp_norm, w_ple, w_ple_gate, ple_norm):
    b, t, d = h.shape
    n = b * t
    da_w = DA_HEADS * 2 * DA_HEAD_DIM
    gqk_w = GDN_QK_HEADS * GDN_DIM
    gv_w = GDN_V_HEADS * GDN_DIM
    conv_ch = 2 * gqk_w + gv_w
    sizes = (da_w, da_w, da_w, conv_ch, gv_w, GDN_V_HEADS, GDN_V_HEADS, d, d)
    offs = [0]
    for s in sizes:
        offs.append(offs[-1] + s)
    row_scale = jnp.ones((offs[-1], 1), F32).at[:da_w].set(math.log2(math.e))
    w_t = (w_in.T * row_scale).astype(BF16)
    n_head = offs[5]
    w_gate_t = w_t[offs[7]:]
    w_ba_t = jnp.pad(w_t[offs[5]:offs[7]], ((0, LANES - 2 * GDN_V_HEADS), (0, 0)))
    col = {"q": 0, "k": da_w, "v": 2 * da_w, "conv": 3 * da_w, "z": 3 * da_w + conv_ch,
           "ga": n_head, "gb": n_head + d}

    x2d = h.reshape(n, d)
    proj, ba = _in_proj(x2d, pre_mix_norm.reshape(1, d), w_t, w_gate_t, w_ba_t, n_head=n_head)
    proj3d = proj.reshape(b, t, -1)

    lambda_init = 0.8 - 0.6 * math.exp(-0.3 * i)
    o_a = _diff_attn(proj3d, lambda_q1.reshape(1, -1), lambda_k1.reshape(1, -1),
                     lambda_q2.reshape(1, -1), lambda_k2.reshape(1, -1), da_sub_norm.reshape(1, -1),
                     q_col=col["q"] // LANES, k_col=col["k"] // LANES, v_col=col["v"] // LANES,
                     lambda_init=lambda_init)

    lanes = jnp.zeros((LANES,), F32)
    a_log = lanes.at[GDN_V_HEADS:2 * GDN_V_HEADS].set(gdn_a_log.astype(F32))
    dt_b = lanes.at[GDN_V_HEADS:2 * GDN_V_HEADS].set(gdn_dt_bias.astype(F32))
    prow = jnp.zeros((8, LANES), F32).at[0].set(a_log).at[1].set(dt_b)
    o_b = _gdn(proj3d, ba.reshape(b, t, LANES), conv_w, prow, gdn_out_norm.reshape(1, -1),
               qkv_col=col["conv"] // GDN_IN_BLOCK, z_col=col["z"] // GDN_IN_BLOCK)

    out = _post(x2d, o_a.reshape(n, -1), o_b.reshape(n, -1), proj, p_i.reshape(n, -1),
                w_branch_a.astype(BF16), w_branch_b.astype(BF16), w_out.astype(BF16),
                post_mix_norm.reshape(1, d), pre_mlp_norm.reshape(1, d), w_up.astype(BF16),
                w_down.astype(BF16), post_mlp_norm.reshape(1, d), w_ple.astype(BF16),
                w_ple_gate.astype(BF16), ple_norm.reshape(1, d),
                ga_col=col["ga"] // d, gb_col=col["gb"] // d)
    return out.reshape(b, t, d)


def kernel(x, p, pre_mix_norm, w_in, conv_w, lambda_q1, lambda_k1, lambda_q2, lambda_k2,
           da_sub_norm, gdn_a_log, gdn_dt_bias, gdn_out_norm, w_branch_a, w_branch_b, w_out,
           post_mix_norm, pre_mlp_norm, w_up, w_down, post_mlp_norm, w_ple, w_ple_gate, ple_norm):
    per_layer = (pre_mix_norm, w_in, conv_w, lambda_q1, lambda_k1, lambda_q2, lambda_k2,
                 da_sub_norm, gdn_a_log, gdn_dt_bias, gdn_out_norm, w_branch_a, w_branch_b, w_out,
                 post_mix_norm, pre_mlp_norm, w_up, w_down, post_mlp_norm, w_ple, w_ple_gate,
                 ple_norm)
    h = x
    for i in range(p.shape[0]):
        h = _layer(h, p[i], i, *(w[i] for w in per_layer))
    return h
```

```python
import functools
import math

import jax
import jax.numpy as jnp
from jax import lax
from jax.experimental import pallas as pl
from jax.experimental.pallas import tpu as pltpu

F32 = jnp.float32
BF16 = jnp.bfloat16

EPS = 1e-6
LANES = 128
DA_HEADS = 8
DA_HEAD_DIM = 64
GDN_QK_HEADS = 8
GDN_V_HEADS = 16
GDN_DIM = 128
CONV_WIDTH = 4
CHUNK = 64
GDN_CPB = 4
GDN_IN_BLOCK = 1024
ATTN_CHAINS = 4
NEG = -1e30
VMEM_LIMIT = 56 * 1024 * 1024

NT_DIMS = (((1,), (1,)), ((), ()))
TN_DIMS = (((0,), (0,)), ((), ()))


def _rms(x, gain):
    return x * lax.rsqrt(jnp.mean(x * x, axis=-1, keepdims=True) + EPS) * gain


def _dot(a, b):
    return jnp.dot(a, b, preferred_element_type=F32)


def _inproj_body(x_ref, gain_ref, w_ref, wg_ref, wba_ref, out_ref, ba_ref, u_scr, *, n_head_blocks):
    j = pl.program_id(1)

    @pl.when(j == 0)
    def _():
        u = _rms(x_ref[...], gain_ref[...]).astype(BF16)
        u_scr[...] = u
        ba_ref[...] = lax.dot_general(u, wba_ref[...], NT_DIMS, preferred_element_type=F32)

    def project(w):
        out_ref[...] = lax.dot_general(u_scr[...], w[...], NT_DIMS,
                                       preferred_element_type=F32).astype(out_ref.dtype)

    pl.when(j < n_head_blocks)(lambda: project(w_ref))
    pl.when(j >= n_head_blocks)(lambda: project(wg_ref))


def _in_proj(x2d, gain, w_t, w_gate_t, w_ba_t, *, n_head, tm=2048, tn=1024):
    n, d = x2d.shape
    n_head_blocks = n_head // tn
    width = n_head + w_gate_t.shape[0]
    return pl.pallas_call(
        functools.partial(_inproj_body, n_head_blocks=n_head_blocks),
        grid=(n // tm, width // tn),
        in_specs=[
            pl.BlockSpec((tm, d), lambda i, j: (i, 0)),
            pl.BlockSpec((1, d), lambda i, j: (0, 0)),
            pl.BlockSpec((tn, d), lambda i, j: (jnp.minimum(j, n_head_blocks - 1), 0)),
            pl.BlockSpec((tn, d), lambda i, j: (jnp.maximum(j - n_head_blocks, 0), 0)),
            pl.BlockSpec((LANES, d), lambda i, j: (0, 0)),
        ],
        out_specs=[
            pl.BlockSpec((tm, tn), lambda i, j: (i, j)),
            pl.BlockSpec((tm, LANES), lambda i, j: (i, 0)),
        ],
        out_shape=[
            jax.ShapeDtypeStruct((n, width), BF16),
            jax.ShapeDtypeStruct((n, LANES), F32),
        ],
        scratch_shapes=[pltpu.VMEM((tm, d), BF16)],
        compiler_params=pltpu.CompilerParams(
            dimension_semantics=("parallel", "arbitrary"), vmem_limit_bytes=VMEM_LIMIT),
        name="in_proj",
    )(x2d, gain, w_t, w_gate_t, w_ba_t)


def _interleave(chains):
    chains = list(chains)
    while chains:
        for g in list(chains):
            if next(g, StopIteration) is StopIteration:
                chains.remove(g)


def _attn_body(lq1_ref, lk1_ref, lq2_ref, lk2_ref, gain_ref, q_ref, k_ref, v_ref, o_ref,
               qt_scr, vt_scr, *, tq, lambda_init):
    t = q_ref.shape[0]
    nq = t // tq
    scale = jnp.asarray(DA_HEAD_DIM ** -0.5, BF16)
    for i in range(nq):
        sl = slice(i * tq, (i + 1) * tq)
        qt_scr[:, sl] = (q_ref[sl, :] * scale).T
        vt_scr[0:LANES, sl] = v_ref[sl, :].T
    vt_scr[LANES:, :] = jnp.ones((vt_scr.shape[0] - LANES, t), BF16)
    lam = (jnp.exp(jnp.sum(lq1_ref[...] * lk1_ref[...], axis=-1, keepdims=True))
           - jnp.exp(jnp.sum(lq2_ref[...] * lk2_ref[...], axis=-1, keepdims=True)) + lambda_init)
    feat = lax.broadcasted_iota(jnp.int32, (LANES, tq), 0)
    kv_row = lax.broadcasted_iota(jnp.int32, (tq, 2 * tq), 0)
    q_col = lax.broadcasted_iota(jnp.int32, (tq, 2 * tq), 1)
    on_or_below_diag = kv_row <= jnp.where(q_col >= tq, q_col - tq, q_col)

    def chain(tiles):
        for qi in tiles:
            qsl = slice(qi * tq, (qi + 1) * tq)
            qt = qt_scr[:, qsl]
            zero = jnp.zeros_like(qt)
            qs = jnp.concatenate([jnp.where(feat < DA_HEAD_DIM, qt, zero),
                                  jnp.where(feat >= DA_HEAD_DIM, qt, zero)], axis=1)
            scores = lambda j: _dot(k_ref[j * tq:(j + 1) * tq, :], qs)
            m = jnp.full((1, 2 * tq), NEG, F32)
            acc = jnp.zeros((vt_scr.shape[0], 2 * tq), F32)
            s_next = scores(0)
            yield
            for j in range(qi + 1):
                s = s_next
                if j < qi:
                    s_next = scores(j + 1)
                    yield
                else:
                    s = jnp.where(on_or_below_diag, s, NEG)
                m_new = jnp.maximum(m, jnp.max(s, axis=0, keepdims=True))
                alpha = jnp.exp2(m - m_new)
                p = jnp.exp2(s - m_new)
                pv = _dot(vt_scr[:, j * tq:(j + 1) * tq], p.astype(BF16))
                yield
                acc = alpha * acc + pv
                m = m_new
            on = acc[0:LANES] * (1.0 / acc[LANES:LANES + 1])
            o = on[:, :tq] - lam * on[:, tq:]
            y = o * lax.rsqrt(jnp.mean(o * o, axis=0, keepdims=True) + EPS)
            o_ref[qsl, :] = (y.T * gain_ref[...] * (1.0 - lambda_init)).astype(o_ref.dtype)

    pairs = [[nq - 1 - i, i] for i in range(nq // 2)]
    for c0 in range(0, len(pairs), ATTN_CHAINS):
        _interleave(chain(tiles) for tiles in pairs[c0:c0 + ATTN_CHAINS])


def _diff_attn(proj3d, lq1, lk1, lq2, lk2, sub_gain, *, q_col, k_col, v_col, lambda_init, tq=256):
    b, t, _ = proj3d.shape
    vec = pl.BlockSpec((1, DA_HEAD_DIM), lambda bi, h: (0, 0))
    head = lambda col: pl.BlockSpec((None, t, LANES), lambda bi, h: (bi, 0, col + h))
    return pl.pallas_call(
        functools.partial(_attn_body, tq=tq, lambda_init=lambda_init),
        grid=(b, DA_HEADS),
        in_specs=[vec, vec, vec, vec, pl.BlockSpec((1, LANES), lambda bi, h: (0, 0)),
                  head(q_col), head(k_col), head(v_col)],
        out_specs=pl.BlockSpec((None, t, LANES), lambda bi, h: (bi, 0, h)),
        out_shape=jax.ShapeDtypeStruct((b, t, DA_HEADS * LANES), BF16),
        scratch_shapes=[pltpu.VMEM((LANES, t), BF16), pltpu.VMEM((LANES + 16, t), BF16)],
        compiler_params=pltpu.CompilerParams(
            dimension_semantics=("parallel", "parallel"), vmem_limit_bytes=VMEM_LIMIT),
        name="diff_attn",
    )(lq1, lk1, lq2, lk2, sub_gain, proj3d, proj3d, proj3d)


def _softplus(x):
    return jnp.maximum(x, 0.0) + jnp.log1p(jnp.exp(-jnp.abs(x)))


def _gdn_body(*refs):
    n_qkv = (2 * GDN_QK_HEADS + GDN_V_HEADS) * GDN_DIM // GDN_IN_BLOCK
    n_z = GDN_V_HEADS * GDN_DIM // GDN_IN_BLOCK
    qkv_refs, z_refs = refs[:n_qkv], refs[n_qkv:n_qkv + n_z]
    (ba_ref, convw_ref, prow_ref, onorm_ref, o_ref,
     xx_scr, state_scr, u_scr, wq_scr, kd_scr, intra_scr, eg_scr, out_scr) = refs[n_qkv + n_z:]
    slabs_per_block = GDN_IN_BLOCK // LANES
    c = CHUNK
    rows = ba_ref.shape[0]
    cis = list(range(rows // c))
    step = pl.program_id(1)
    wslot = step % 2
    rslot = 1 - wslot
    rep = GDN_V_HEADS // GDN_QK_HEADS
    hks = [(ci, hk) for ci in cis for hk in range(GDN_QK_HEADS)]
    hs = [(ci, h) for ci in cis for h in range(GDN_V_HEADS)]
    fh = lambda ci, h: ci * GDN_V_HEADS + h
    fk = lambda ci, hk: ci * GDN_QK_HEADS + hk
    bf = lambda y: y.astype(BF16)

    def load_block(first_block):
        if first_block:
            xx_scr[:, 0:8, :] = jnp.zeros((xx_scr.shape[0], 8, LANES), F32)
            state_scr[...] = jnp.zeros(state_scr.shape, F32)
        else:
            xx_scr[:, 0:8, :] = xx_scr[:, rows:rows + 8, :]
        for slab in range(xx_scr.shape[0]):
            blk, off = divmod(slab, slabs_per_block)
            xx_scr[slab, 8:8 + rows, :] = qkv_refs[blk][:, off * LANES:(off + 1) * LANES].astype(F32)

    def row_blocks():
        return [(32 * g + r) for g in range(c // 32) for r in range(4)]

    def perm_time(i):
        return ((i >> 5) << 5) + ((i & 7) << 2) + ((i >> 3) & 3)

    def conv_silu(ci, slab):
        sl = slice(slab * LANES, (slab + 1) * LANES)
        w = [convw_ref[k:k + 1, sl] for k in range(CONV_WIDTH)]
        outs = []
        for g in range(c // 32):
            z = {rho: xx_scr[slab, pl.ds(8 + c * ci + 32 * g + rho, 8, stride=4), :]
                 for rho in range(1 - CONV_WIDTH, 4)}
            for r in range(4):
                y = w[CONV_WIDTH - 1] * z[r]
                for d in range(1, CONV_WIDTH):
                    y = y + w[CONV_WIDTH - 1 - d] * z[r - d]
                outs.append(y)
        y = jnp.concatenate(outs, axis=0)
        return y * jax.nn.sigmoid(y)

    def l2n(y):
        return y * lax.rsqrt(jnp.sum(y * y, axis=-1, keepdims=True) + EPS)

    lane = lax.broadcasted_iota(jnp.int32, (c, 2 * c), 1)
    first = lane < c
    zeros_c = jnp.zeros((c, 2 * c), BF16)

    def blockdiag(x):
        xb = bf(x)
        return jnp.concatenate([jnp.where(first, xb, zeros_c), jnp.where(first, zeros_c, xb)], axis=0)

    def recurrence():
        s = {h: state_scr[h] for h in range(GDN_V_HEADS)}
        o = {}
        for ci in cis:
            heads = range(GDN_V_HEADS)
            ws = {h: _dot(wq_scr[rslot, fh(ci, h)], bf(s[h])) for h in heads}
            yield
            vnb = {h: bf(u_scr[rslot, fh(ci, h)] - ws[h][:c]) for h in heads}
            zv = jnp.zeros((c, GDN_DIM), BF16)
            for hk in range(GDN_QK_HEADS):
                a_, b_ = rep * hk, rep * hk + 1
                vbd = jnp.concatenate([jnp.concatenate([vnb[a_], zv], axis=1),
                                       jnp.concatenate([zv, vnb[b_]], axis=1)], axis=0)
                ov = _dot(intra_scr[rslot, fk(ci, hk)], vbd)
                o[ci, a_] = ws[a_][c:] + ov[:, :GDN_DIM]
                o[ci, b_] = ws[b_][c:] + ov[:, GDN_DIM:]
            s = {h: s[h] * eg_scr[rslot, fh(ci, h)][0:1, :] + lax.dot_general(
                kd_scr[rslot, fh(ci, h)], vnb[h], TN_DIMS, preferred_element_type=F32) for h in heads}
            yield
        for h in range(GDN_V_HEADS):
            state_scr[h] = s[h]
        for ci, h in hs:
            y = _rms(o[ci, h], onorm_ref[...])
            for m, row0 in enumerate(row_blocks()):
                out_scr[fh(ci, h), pl.ds(row0, 8, stride=4), :] = y[8 * m:8 * m + 8]
        for ci, h in hs:
            blk, off = divmod(h, slabs_per_block)
            z = z_refs[blk][c * ci:c * (ci + 1), off * LANES:(off + 1) * LANES].astype(F32)
            o_ref[c * ci:c * (ci + 1), h * GDN_DIM:(h + 1) * GDN_DIM] = (
                out_scr[fh(ci, h)] * (z * jax.nn.sigmoid(z))).astype(o_ref.dtype)

    def prep(cis):
        hks = [(ci, hk) for ci in cis for hk in range(GDN_QK_HEADS)]
        hs = [(ci, h) for ci in cis for h in range(GDN_V_HEADS)]
        ii = perm_time(lax.broadcasted_iota(jnp.int32, (c, c), 0))
        jj = perm_time(lax.broadcasted_iota(jnp.int32, (c, c), 1))
        row2 = perm_time(lax.broadcasted_iota(jnp.int32, (c, 2 * c), 0))
        col2 = perm_time(jnp.where(first, lane, lane - c))
        strict = row2 > col2
        causal = row2 >= col2
        hp = lax.Precision.HIGHEST
        dup = lambda m: jnp.concatenate([m, m], axis=1)
        tril = jnp.where(ii >= jj, 1.0, 0.0).astype(F32)
        triu2 = dup(jnp.where(ii <= jj, 1.0, 0.0).astype(F32))
        eye2 = dup(jnp.where(ii == jj, 1.0, 0.0).astype(F32))
        gc_row, beta_row, gcol, egc, kdf = {}, {}, {}, {}, {}
        for ci in cis:
            ba = jnp.concatenate([ba_ref[pl.ds(c * ci + row0, 8, stride=4), :]
                                  for row0 in row_blocks()], axis=0)
            beta_col = jax.nn.sigmoid(ba)
            g_col = -jnp.exp(prow_ref[0:1, :]) * _softplus(ba + prow_ref[1:2, :])
            gc_col = jnp.dot(tril, g_col, precision=hp, preferred_element_type=F32)
            gc_row[ci] = lax.dot_general(g_col, triu2, TN_DIMS, precision=hp,
                                         preferred_element_type=F32)
            beta_row[ci] = lax.dot_general(beta_col, eye2, TN_DIMS, precision=hp,
                                           preferred_element_type=F32)
            g_last = gc_col[c - 1:c, :]
            egc_all = jnp.exp(gc_col)
            kdf_all = pltpu.roll(beta_col, GDN_V_HEADS, axis=1) * jnp.exp(g_last - gc_col)
            eg_all = jnp.exp(g_last)
            for h in range(GDN_V_HEADS):
                ln = slice(GDN_V_HEADS + h, GDN_V_HEADS + h + 1)
                gcol[ci, h] = jnp.broadcast_to(gc_col[:, ln], (c, 2 * c))
                egc[ci, h] = jnp.broadcast_to(egc_all[:, ln], (c, GDN_DIM))
                kdf[ci, h] = jnp.broadcast_to(kdf_all[:, ln], (c, GDN_DIM))
                eg_scr[wslot, fh(ci, h)] = jnp.broadcast_to(eg_all[:, ln], eg_scr.shape[2:])
        q = {(ci, hk): l2n(conv_silu(ci, hk)) * (GDN_DIM ** -0.5) for ci, hk in hks}
        k = {(ci, hk): l2n(conv_silu(ci, GDN_QK_HEADS + hk)) for ci, hk in hks}
        gram = {x: lax.dot_general(jnp.concatenate([bf(k[x]), bf(q[x])], axis=0),
                                   jnp.concatenate([bf(k[x]), bf(k[x])], axis=0),
                                   NT_DIMS, preferred_element_type=F32) for x in hks}
        yield
        pick = lambda m, hk: jnp.where(first[0:1, :], m[rep * hk:rep * hk + 1, :],
                                       m[rep * hk + 1:rep * hk + 2, :])
        a = {}
        for ci, hk in hks:
            gcol2 = jnp.where(first, gcol[ci, rep * hk], gcol[ci, rep * hk + 1])
            dec = jnp.where(causal, jnp.exp(gcol2 - pick(gc_row[ci][GDN_V_HEADS:], hk)), 0.0)
            brow = pick(beta_row[ci], hk)
            a[ci, hk] = jnp.where(strict, gram[ci, hk][:c] * dec, 0.0) * brow
            intra_scr[wslot, fk(ci, hk)] = bf(gram[ci, hk][c:] * dec * brow)
        n = {x: -a[x] for x in hks}
        p = {x: _dot(bf(a[x]), blockdiag(a[x])) for x in hks}
        yield
        for _ in range(4):
            r_ = {x: _dot(jnp.concatenate([bf(n[x]), bf(p[x])], axis=0), blockdiag(p[x])) for x in hks}
            yield
            n = {x: n[x] + p[x] + r_[x][:c] for x in hks}
            p = {x: r_[x][c:] for x in hks}
        n = {x: n[x] + p[x] + _dot(bf(n[x]), blockdiag(p[x])) for x in hks}
        yield
        rhs = {(ci, h): jnp.concatenate([conv_silu(ci, 2 * GDN_QK_HEADS + h),
                                         k[ci, h // rep] * egc[ci, h]], axis=1) for ci, h in hs}
        zr = jnp.zeros((c, 2 * GDN_DIM), BF16)
        for ci, hk in hks:
            a_, b_ = rep * hk, rep * hk + 1
            rbd = jnp.concatenate([jnp.concatenate([bf(rhs[ci, a_]), zr], axis=1),
                                   jnp.concatenate([zr, bf(rhs[ci, b_])], axis=1)], axis=0)
            uw = (jnp.concatenate([rhs[ci, a_], rhs[ci, b_]], axis=1)
                  + _dot(bf(n[ci, hk]), rbd))
            for i_, h in enumerate((a_, b_)):
                o0 = 2 * GDN_DIM * i_
                u_scr[wslot, fh(ci, h)] = uw[:, o0:o0 + GDN_DIM]
                wq_scr[wslot, fh(ci, h)] = jnp.concatenate(
                    [bf(uw[:, o0 + GDN_DIM:o0 + 2 * GDN_DIM]), bf(q[ci, hk] * egc[ci, h])], axis=0)
                kd_scr[wslot, fh(ci, h)] = bf(k[ci, hk] * kdf[ci, h])

    last = pl.num_programs(1) - 1

    @pl.when(step == 0)
    def _():
        load_block(True)
        _interleave([prep([ci]) for ci in cis])

    @pl.when((step > 0) & (step < last))
    def _():
        load_block(False)
        _interleave([recurrence()] + [prep([ci]) for ci in cis])

    @pl.when(step == last)
    def _():
        _interleave([recurrence()])


def _gdn(proj3d, ba3d, conv_w, prow, onorm, *, qkv_col, z_col):
    b, t, _ = proj3d.shape
    c = CHUNK
    rows = c * GDN_CPB
    nt = t // rows
    conv_ch = conv_w.shape[1]
    v_width = GDN_V_HEADS * GDN_DIM
    cur = lambda bi, si: (bi, jnp.minimum(si, nt - 1))
    prev = lambda bi, si: (bi, jnp.maximum(si - 1, 0))
    const = lambda bi, si: (0, 0)
    heads = (2, GDN_CPB * GDN_V_HEADS)
    return pl.pallas_call(
        _gdn_body,
        grid=(b, nt + 1),
        in_specs=[
            *[pl.BlockSpec((None, rows, GDN_IN_BLOCK), lambda bi, si, j=j: cur(bi, si) + (qkv_col + j,))
              for j in range(conv_ch // GDN_IN_BLOCK)],
            *[pl.BlockSpec((None, rows, GDN_IN_BLOCK), lambda bi, si, j=j: prev(bi, si) + (z_col + j,))
              for j in range(v_width // GDN_IN_BLOCK)],
            pl.BlockSpec((None, rows, LANES), lambda bi, si: cur(bi, si) + (0,)),
            pl.BlockSpec((CONV_WIDTH, conv_ch), const),
            pl.BlockSpec((8, LANES), const),
            pl.BlockSpec((1, GDN_DIM), const),
        ],
        out_specs=pl.BlockSpec((None, rows, v_width), lambda bi, si: prev(bi, si) + (0,)),
        out_shape=jax.ShapeDtypeStruct((b, t, v_width), BF16),
        scratch_shapes=[pltpu.VMEM((conv_ch // LANES, rows + 8, LANES), F32),
                        pltpu.VMEM((GDN_V_HEADS, GDN_DIM, GDN_DIM), F32),
                        pltpu.VMEM(heads + (c, GDN_DIM), F32),
                        pltpu.VMEM(heads + (2 * c, GDN_DIM), BF16),
                        pltpu.VMEM(heads + (c, GDN_DIM), BF16),
                        pltpu.VMEM((2, GDN_CPB * GDN_QK_HEADS, c, 2 * c), BF16),
                        pltpu.VMEM(heads + (8, LANES), F32),
                        pltpu.VMEM(heads[1:] + (c, GDN_DIM), F32)],
        compiler_params=pltpu.CompilerParams(
            dimension_semantics=("parallel", "arbitrary"), vmem_limit_bytes=VMEM_LIMIT),
        name="gdn",
    )(*([proj3d] * ((conv_ch + v_width) // GDN_IN_BLOCK)), ba3d, conv_w, prow, onorm)


def _post_body(x_ref, oa_ref, ob_ref, ga_ref, gb_ref, p_ref, wa_ref, wb_ref, wo_ref, g_mix_ref,
               g_pre_ref, wup_ref, wdn_ref, g_post_ref, wple_ref, wgate_ref, g_ple_ref, out_ref,
               *, ff_chunk):
    ya = _dot(oa_ref[...], wa_ref[...])
    yb = _dot(ob_ref[...], wb_ref[...])
    merged = (jax.nn.sigmoid(ga_ref[...].astype(F32)) * ya
              + jax.nn.sigmoid(gb_ref[...].astype(F32)) * yb)
    h = x_ref[...] + _rms(_dot(merged.astype(BF16), wo_ref[...]), g_mix_ref[...])
    u = _rms(h, g_pre_ref[...]).astype(BF16)
    d_ff = wup_ref.shape[1]
    acc = jnp.zeros(h.shape, F32)
    for c0 in range(0, d_ff, ff_chunk):
        hid = jnp.square(jnp.maximum(_dot(u, wup_ref[:, c0:c0 + ff_chunk]), 0.0))
        acc = acc + _dot(hid.astype(BF16), wdn_ref[c0:c0 + ff_chunk, :])
    h = h + _rms(acc, g_post_ref[...])
    e = _dot(p_ref[...].astype(BF16), wple_ref[...]) * jax.nn.sigmoid(
        _dot(h.astype(BF16), wgate_ref[...]))
    out_ref[...] = h + _rms(e, g_ple_ref[...])


def _post(x2d, o_a, o_b, proj, p2d, w_a, w_b, w_o, g_mix, g_pre, w_up, w_dn, g_post, w_ple, w_gate,
          g_ple, *, ga_col, gb_col, tm=512, ff_chunk=1024):
    n, d = x2d.shape
    row = lambda i: (i, 0)
    const = lambda i: (0, 0)
    weight = lambda w: pl.BlockSpec(w.shape, const, pipeline_mode=pl.Buffered(1))
    gain = pl.BlockSpec((1, d), const)
    return pl.pallas_call(
        functools.partial(_post_body, ff_chunk=ff_chunk),
        grid=(n // tm,),
        in_specs=[
            pl.BlockSpec((tm, d), row),
            pl.BlockSpec((tm, o_a.shape[1]), row),
            pl.BlockSpec((tm, o_b.shape[1]), row),
            pl.BlockSpec((tm, d), lambda i: (i, ga_col)),
            pl.BlockSpec((tm, d), lambda i: (i, gb_col)),
            pl.BlockSpec((tm, p2d.shape[1]), row),
            weight(w_a), weight(w_b), weight(w_o), gain,
            gain, weight(w_up), weight(w_dn), gain,
            weight(w_ple), weight(w_gate), gain,
        ],
        out_specs=pl.BlockSpec((tm, d), row),
        out_shape=jax.ShapeDtypeStruct((n, d), F32),
        compiler_params=pltpu.CompilerParams(
            dimension_semantics=("parallel",), vmem_limit_bytes=VMEM_LIMIT),
        name="post",
    )(x2d, o_a, o_b, proj, proj, p2d, w_a, w_b, w_o, g_mix, g_pre, w_up, w_dn, g_post, w_ple,
      w_gate, g_ple)


def _layer(h, p_i, i, pre_mix_norm, w_in, conv_w, lambda_q1, lambda_k1, lambda_q2, lambda_k2,
           da_sub_norm, gdn_a_log, gdn_dt_bias, gdn_out_norm, w_branch_a, w_branch_b, w_out,
           post_mix_norm, pre_mlp_norm, w_up, w_down, post_mlp_norm, w_ple, w_ple_gate, ple_norm):
    b, t, d = h.shape
    n = b * t
    da_w = DA_HEADS * 2 * DA_HEAD_DIM
    gqk_w = GDN_QK_HEADS * GDN_DIM
    gv_w = GDN_V_HEADS * GDN_DIM
    conv_ch = 2 * gqk_w + gv_w
    sizes = (da_w, da_w, da_w, conv_ch, gv_w, GDN_V_HEADS, GDN_V_HEADS, d, d)
    offs = [0]
    for s in sizes:
        offs.append(offs[-1] + s)
    row_scale = jnp.ones((offs[-1], 1), F32).at[:da_w].set(math.log2(math.e))
    w_t = (w_in.T * row_scale).astype(BF16)
    n_head = offs[5]
    w_gate_t = w_t[offs[7]:]
    w_ba_t = jnp.pad(w_t[offs[5]:offs[7]], ((0, LANES - 2 * GDN_V_HEADS), (0, 0)))
    col = {"q": 0, "k": da_w, "v": 2 * da_w, "conv": 3 * da_w, "z": 3 * da_w + conv_ch,
           "ga": n_head, "gb": n_head + d}

    x2d = h.reshape(n, d)
    proj, ba = _in_proj(x2d, pre_mix_norm.reshape(1, d), w_t, w_gate_t, w_ba_t, n_head=n_head)
    proj3d = proj.reshape(b, t, -1)

    lambda_init = 0.8 - 0.6 * math.exp(-0.3 * i)
    o_a = _diff_attn(proj3d, lambda_q1.reshape(1, -1), lambda_k1.reshape(1, -1),
                     lambda_q2.reshape(1, -1), lambda_k2.reshape(1, -1), da_sub_norm.reshape(1, -1),
                     q_col=col["q"] // LANES, k_col=col["k"] // LANES, v_col=col["v"] // LANES,
                     lambda_init=lambda_init)

    lanes = jnp.zeros((LANES,), F32)
    a_log = lanes.at[GDN_V_HEADS:2 * GDN_V_HEADS].set(gdn_a_log.astype(F32))
    dt_b = lanes.at[GDN_V_HEADS:2 * GDN_V_HEADS].set(gdn_dt_bias.astype(F32))
    prow = jnp.zeros((8, LANES), F32).at[0].set(a_log).at[1].set(dt_b)
    o_b = _gdn(proj3d, ba.reshape(b, t, LANES), conv_w, prow, gdn_out_norm.reshape(1, -1),
               qkv_col=col["conv"] // GDN_IN_BLOCK, z_col=col["z"] // GDN_IN_BLOCK)

    out = _post(x2d, o_a.reshape(n, -1), o_b.reshape(n, -1), proj, p_i.reshape(n, -1),
                w_branch_a.astype(BF16), w_branch_b.astype(BF16), w_out.astype(BF16),
                post_mix_norm.reshape(1, d), pre_mlp_norm.reshape(1, d), w_up.astype(BF16),
                w_down.astype(BF16), post_mlp_norm.reshape(1, d), w_ple.astype(BF16),
                w_ple_gate.astype(BF16), ple_norm.reshape(1, d),
                ga_col=col["ga"] // d, gb_col=col["gb"] // d)
    return out.reshape(b, t, d)


def kernel(x, p, pre_mix_norm, w_in, conv_w, lambda_q1, lambda_k1, lambda_q2, lambda_k2,
           da_sub_norm, gdn_a_log, gdn_dt_bias, gdn_out_norm, w_branch_a, w_branch_b, w_out,
           post_mix_norm, pre_mlp_norm, w_up, w_down, post_mlp_norm, w_ple, w_ple_gate, ple_norm):
    per_layer = (pre_mix_norm, w_in, conv_w, lambda_q1, lambda_k1, lambda_q2, lambda_k2,
                 da_sub_norm, gdn_a_log, gdn_dt_bias, gdn_out_norm, w_branch_a, w_branch_b, w_out,
                 post_mix_norm, pre_mlp_norm, w_up, w_down, post_mlp_norm, w_ple, w_ple_gate,
                 ple_norm)
    h = x
    for i in range(p.shape[0]):
        h = _layer(h, p[i], i, *(w[i] for w in per_layer))
    return h
```

```python
import functools
import math

import jax
import jax.numpy as jnp
from jax import lax
from jax.experimental import pallas as pl
from jax.experimental.pallas import tpu as pltpu

F32 = jnp.float32
BF16 = jnp.bfloat16

EPS = 1e-6
LANES = 128
SUBLANES = 8
BF16_ROWS = 16
DA_HEADS = 8
DA_HEAD_DIM = 64
GDN_QK_HEADS = 8
GDN_V_HEADS = 16
GDN_DIM = 128
CONV_WIDTH = 4
CHUNK = 64
GDN_CPB = 4
ROW_STRIDE = 4
ROW_GROUP = ROW_STRIDE * SUBLANES
GDN_IN_BLOCK = 1024
ATTN_CHAINS = 4
NEG = -1e30
VMEM_LIMIT = 56 * 1024 * 1024

NT_DIMS = (((1,), (1,)), ((), ()))
TN_DIMS = (((0,), (0,)), ((), ()))


def _rms(x, gain):
    return x * lax.rsqrt(jnp.mean(x * x, axis=-1, keepdims=True) + EPS) * gain


def _dot(a, b):
    return jnp.dot(a, b, preferred_element_type=F32)


def _inproj_body(x_ref, gain_ref, w_ref, wg_ref, wba_ref, out_ref, ba_ref, u_scr, *, n_head_blocks):
    j = pl.program_id(1)

    @pl.when(j == 0)
    def _():
        u = _rms(x_ref[...], gain_ref[...]).astype(BF16)
        u_scr[...] = u
        ba_ref[...] = lax.dot_general(u, wba_ref[...], NT_DIMS, preferred_element_type=F32)

    def project(w):
        out_ref[...] = lax.dot_general(u_scr[...], w[...], NT_DIMS,
                                       preferred_element_type=F32).astype(out_ref.dtype)

    pl.when(j < n_head_blocks)(lambda: project(w_ref))
    pl.when(j >= n_head_blocks)(lambda: project(wg_ref))


def _in_proj(x2d, gain, w_t, w_gate_t, w_ba_t, *, n_head, tm=2048, tn=1024):
    n, d = x2d.shape
    n_head_blocks = n_head // tn
    width = n_head + w_gate_t.shape[0]
    return pl.pallas_call(
        functools.partial(_inproj_body, n_head_blocks=n_head_blocks),
        grid=(n // tm, width // tn),
        in_specs=[
            pl.BlockSpec((tm, d), lambda i, j: (i, 0)),
            pl.BlockSpec((1, d), lambda i, j: (0, 0)),
            pl.BlockSpec((tn, d), lambda i, j: (jnp.minimum(j, n_head_blocks - 1), 0)),
            pl.BlockSpec((tn, d), lambda i, j: (jnp.maximum(j - n_head_blocks, 0), 0)),
            pl.BlockSpec((LANES, d), lambda i, j: (0, 0)),
        ],
        out_specs=[
            pl.BlockSpec((tm, tn), lambda i, j: (i, j)),
            pl.BlockSpec((tm, LANES), lambda i, j: (i, 0)),
        ],
        out_shape=[
            jax.ShapeDtypeStruct((n, width), BF16),
            jax.ShapeDtypeStruct((n, LANES), F32),
        ],
        scratch_shapes=[pltpu.VMEM((tm, d), BF16)],
        compiler_params=pltpu.CompilerParams(
            dimension_semantics=("parallel", "arbitrary"), vmem_limit_bytes=VMEM_LIMIT),
        name="in_proj",
    )(x2d, gain, w_t, w_gate_t, w_ba_t)


def _interleave(chains):
    chains = list(chains)
    while chains:
        for g in list(chains):
            if next(g, StopIteration) is StopIteration:
                chains.remove(g)


def _attn_body(lq1_ref, lk1_ref, lq2_ref, lk2_ref, gain_ref, q_ref, k_ref, v_ref, o_ref,
               qt_scr, vt_scr, *, tq, lambda_init):
    t = q_ref.shape[0]
    nq = t // tq
    scale = jnp.asarray(DA_HEAD_DIM ** -0.5, BF16)
    for i in range(nq):
        sl = slice(i * tq, (i + 1) * tq)
        qt_scr[:, sl] = (q_ref[sl, :] * scale).T
        vt_scr[0:LANES, sl] = v_ref[sl, :].T
    vt_scr[LANES:, :] = jnp.ones((vt_scr.shape[0] - LANES, t), BF16)
    lam = (jnp.exp(jnp.sum(lq1_ref[...] * lk1_ref[...], axis=-1, keepdims=True))
           - jnp.exp(jnp.sum(lq2_ref[...] * lk2_ref[...], axis=-1, keepdims=True)) + lambda_init)
    feat = lax.broadcasted_iota(jnp.int32, (LANES, tq), 0)
    kv_row = lax.broadcasted_iota(jnp.int32, (tq, 2 * tq), 0)
    q_col = lax.broadcasted_iota(jnp.int32, (tq, 2 * tq), 1)
    on_or_below_diag = kv_row <= jnp.where(q_col >= tq, q_col - tq, q_col)

    def chain(tiles):
        for qi in tiles:
            qsl = slice(qi * tq, (qi + 1) * tq)
            qt = qt_scr[:, qsl]
            zero = jnp.zeros_like(qt)
            qs = jnp.concatenate([jnp.where(feat < DA_HEAD_DIM, qt, zero),
                                  jnp.where(feat >= DA_HEAD_DIM, qt, zero)], axis=1)
            scores = lambda j: _dot(k_ref[j * tq:(j + 1) * tq, :], qs)
            m = jnp.full((1, 2 * tq), NEG, F32)
            acc = jnp.zeros((vt_scr.shape[0], 2 * tq), F32)
            s_next = scores(0)
            yield
            for j in range(qi + 1):
                s = s_next
                if j < qi:
                    s_next = scores(j + 1)
                    yield
                else:
                    s = jnp.where(on_or_below_diag, s, NEG)
                m_new = jnp.maximum(m, jnp.max(s, axis=0, keepdims=True))
                alpha = jnp.exp2(m - m_new)
                p = jnp.exp2(s - m_new)
                pv = _dot(vt_scr[:, j * tq:(j + 1) * tq], p.astype(BF16))
                yield
                acc = alpha * acc + pv
                m = m_new
            on = acc[0:LANES] * (1.0 / acc[LANES:LANES + 1])
            o = on[:, :tq] - lam * on[:, tq:]
            y = o * lax.rsqrt(jnp.mean(o * o, axis=0, keepdims=True) + EPS)
            o_ref[qsl, :] = (y.T * gain_ref[...] * (1.0 - lambda_init)).astype(o_ref.dtype)

    pairs = [[nq - 1 - i, i] for i in range(nq // 2)]
    for c0 in range(0, len(pairs), ATTN_CHAINS):
        _interleave(chain(tiles) for tiles in pairs[c0:c0 + ATTN_CHAINS])


def _diff_attn(proj3d, lq1, lk1, lq2, lk2, sub_gain, *, q_col, k_col, v_col, lambda_init, tq=256):
    b, t, _ = proj3d.shape
    vec = pl.BlockSpec((1, DA_HEAD_DIM), lambda bi, h: (0, 0))
    head = lambda col: pl.BlockSpec((None, t, LANES), lambda bi, h: (bi, 0, col + h))
    return pl.pallas_call(
        functools.partial(_attn_body, tq=tq, lambda_init=lambda_init),
        grid=(b, DA_HEADS),
        in_specs=[vec, vec, vec, vec, pl.BlockSpec((1, LANES), lambda bi, h: (0, 0)),
                  head(q_col), head(k_col), head(v_col)],
        out_specs=pl.BlockSpec((None, t, LANES), lambda bi, h: (bi, 0, h)),
        out_shape=jax.ShapeDtypeStruct((b, t, DA_HEADS * LANES), BF16),
        scratch_shapes=[pltpu.VMEM((LANES, t), BF16), pltpu.VMEM((LANES + BF16_ROWS, t), BF16)],
        compiler_params=pltpu.CompilerParams(
            dimension_semantics=("parallel", "parallel"), vmem_limit_bytes=VMEM_LIMIT),
        name="diff_attn",
    )(lq1, lk1, lq2, lk2, sub_gain, proj3d, proj3d, proj3d)


def _softplus(x):
    return jnp.maximum(x, 0.0) + jnp.log1p(jnp.exp(-jnp.abs(x)))


def _gdn_body(*refs):
    n_qkv = (2 * GDN_QK_HEADS + GDN_V_HEADS) * GDN_DIM // GDN_IN_BLOCK
    n_z = GDN_V_HEADS * GDN_DIM // GDN_IN_BLOCK
    qkv_refs, z_refs = refs[:n_qkv], refs[n_qkv:n_qkv + n_z]
    (ba_ref, convw_ref, prow_ref, onorm_ref, o_ref,
     xx_scr, state_scr, u_scr, wq_scr, kd_scr, intra_scr, eg_scr, out_scr) = refs[n_qkv + n_z:]
    slabs_per_block = GDN_IN_BLOCK // LANES
    c = CHUNK
    rows = ba_ref.shape[0]
    cis = list(range(rows // c))
    step = pl.program_id(1)
    wslot = step % 2
    rslot = 1 - wslot
    rep = GDN_V_HEADS // GDN_QK_HEADS
    hks = [(ci, hk) for ci in cis for hk in range(GDN_QK_HEADS)]
    hs = [(ci, h) for ci in cis for h in range(GDN_V_HEADS)]
    fh = lambda ci, h: ci * GDN_V_HEADS + h
    fk = lambda ci, hk: ci * GDN_QK_HEADS + hk
    bf = lambda y: y.astype(BF16)

    halo = SUBLANES

    def load_block(first_block):
        if first_block:
            xx_scr[:, 0:halo, :] = jnp.zeros((xx_scr.shape[0], halo, LANES), F32)
            state_scr[...] = jnp.zeros(state_scr.shape, F32)
        else:
            xx_scr[:, 0:halo, :] = xx_scr[:, rows:rows + halo, :]
        for slab in range(xx_scr.shape[0]):
            blk, off = divmod(slab, slabs_per_block)
            xx_scr[slab, halo:halo + rows, :] = (
                qkv_refs[blk][:, off * LANES:(off + 1) * LANES].astype(F32))

    def row_blocks():
        return [ROW_GROUP * g + r for g in range(c // ROW_GROUP) for r in range(ROW_STRIDE)]

    def perm_time(i):
        sub_bits, stride_bits = SUBLANES.bit_length() - 1, ROW_STRIDE.bit_length() - 1
        group = i >> (sub_bits + stride_bits)
        return ((group << (sub_bits + stride_bits)) + ((i & (SUBLANES - 1)) << stride_bits)
                + ((i >> sub_bits) & (ROW_STRIDE - 1)))

    def conv_silu(ci, slab):
        sl = slice(slab * LANES, (slab + 1) * LANES)
        w = [convw_ref[k:k + 1, sl] for k in range(CONV_WIDTH)]
        outs = []
        for g in range(c // ROW_GROUP):
            z = {rho: xx_scr[slab, pl.ds(halo + c * ci + ROW_GROUP * g + rho, SUBLANES,
                                         stride=ROW_STRIDE), :]
                 for rho in range(1 - CONV_WIDTH, ROW_STRIDE)}
            for r in range(ROW_STRIDE):
                y = w[CONV_WIDTH - 1] * z[r]
                for d in range(1, CONV_WIDTH):
                    y = y + w[CONV_WIDTH - 1 - d] * z[r - d]
                outs.append(y)
        y = jnp.concatenate(outs, axis=0)
        return y * jax.nn.sigmoid(y)

    def l2n(y):
        return y * lax.rsqrt(jnp.sum(y * y, axis=-1, keepdims=True) + EPS)

    lane = lax.broadcasted_iota(jnp.int32, (c, 2 * c), 1)
    first = lane < c
    zeros_c = jnp.zeros((c, 2 * c), BF16)

    def blockdiag(x):
        xb = bf(x)
        return jnp.concatenate([jnp.where(first, xb, zeros_c), jnp.where(first, zeros_c, xb)], axis=0)

    def recurrence():
        s = {h: state_scr[h] for h in range(GDN_V_HEADS)}
        o = {}
        for ci in cis:
            heads = range(GDN_V_HEADS)
            ws = {h: _dot(wq_scr[rslot, fh(ci, h)], bf(s[h])) for h in heads}
            yield
            vnb = {h: bf(u_scr[rslot, fh(ci, h)] - ws[h][:c]) for h in heads}
            zv = jnp.zeros((c, GDN_DIM), BF16)
            for hk in range(GDN_QK_HEADS):
                a_, b_ = rep * hk, rep * hk + 1
                vbd = jnp.concatenate([jnp.concatenate([vnb[a_], zv], axis=1),
                                       jnp.concatenate([zv, vnb[b_]], axis=1)], axis=0)
                ov = _dot(intra_scr[rslot, fk(ci, hk)], vbd)
                o[ci, a_] = ws[a_][c:] + ov[:, :GDN_DIM]
                o[ci, b_] = ws[b_][c:] + ov[:, GDN_DIM:]
            s = {h: s[h] * eg_scr[rslot, fh(ci, h)][0:1, :] + lax.dot_general(
                kd_scr[rslot, fh(ci, h)], vnb[h], TN_DIMS, preferred_element_type=F32) for h in heads}
            yield
        for h in range(GDN_V_HEADS):
            state_scr[h] = s[h]
        for ci, h in hs:
            y = _rms(o[ci, h], onorm_ref[...])
            for m, row0 in enumerate(row_blocks()):
                out_scr[fh(ci, h), pl.ds(row0, SUBLANES, stride=ROW_STRIDE), :] = (
                    y[SUBLANES * m:SUBLANES * (m + 1)])
        for ci, h in hs:
            blk, off = divmod(h, slabs_per_block)
            z = z_refs[blk][c * ci:c * (ci + 1), off * LANES:(off + 1) * LANES].astype(F32)
            o_ref[c * ci:c * (ci + 1), h * GDN_DIM:(h + 1) * GDN_DIM] = (
                out_scr[fh(ci, h)] * (z * jax.nn.sigmoid(z))).astype(o_ref.dtype)

    def prep(cis):
        hks = [(ci, hk) for ci in cis for hk in range(GDN_QK_HEADS)]
        hs = [(ci, h) for ci in cis for h in range(GDN_V_HEADS)]
        ii = perm_time(lax.broadcasted_iota(jnp.int32, (c, c), 0))
        jj = perm_time(lax.broadcasted_iota(jnp.int32, (c, c), 1))
        row2 = perm_time(lax.broadcasted_iota(jnp.int32, (c, 2 * c), 0))
        col2 = perm_time(jnp.where(first, lane, lane - c))
        strict = row2 > col2
        causal = row2 >= col2
        hp = lax.Precision.HIGHEST
        dup = lambda m: jnp.concatenate([m, m], axis=1)
        tril = jnp.where(ii >= jj, 1.0, 0.0).astype(F32)
        triu2 = dup(jnp.where(ii <= jj, 1.0, 0.0).astype(F32))
        eye2 = dup(jnp.where(ii == jj, 1.0, 0.0).astype(F32))
        gc_row, beta_row, gcol, egc, kdf = {}, {}, {}, {}, {}
        for ci in cis:
            ba = jnp.concatenate([ba_ref[pl.ds(c * ci + row0, SUBLANES, stride=ROW_STRIDE), :]
                                  for row0 in row_blocks()], axis=0)
            beta_col = jax.nn.sigmoid(ba)
            g_col = -jnp.exp(prow_ref[0:1, :]) * _softplus(ba + prow_ref[1:2, :])
            gc_col = jnp.dot(tril, g_col, precision=hp, preferred_element_type=F32)
            gc_row[ci] = lax.dot_general(g_col, triu2, TN_DIMS, precision=hp,
                                         preferred_element_type=F32)
            beta_row[ci] = lax.dot_general(beta_col, eye2, TN_DIMS, precision=hp,
                                           preferred_element_type=F32)
            g_last = gc_col[c - 1:c, :]
            egc_all = jnp.exp(gc_col)
            kdf_all = pltpu.roll(beta_col, GDN_V_HEADS, axis=1) * jnp.exp(g_last - gc_col)
            eg_all = jnp.exp(g_last)
            for h in range(GDN_V_HEADS):
                ln = slice(GDN_V_HEADS + h, GDN_V_HEADS + h + 1)
                gcol[ci, h] = jnp.broadcast_to(gc_col[:, ln], (c, 2 * c))
                egc[ci, h] = jnp.broadcast_to(egc_all[:, ln], (c, GDN_DIM))
                kdf[ci, h] = jnp.broadcast_to(kdf_all[:, ln], (c, GDN_DIM))
                eg_scr[wslot, fh(ci, h)] = jnp.broadcast_to(eg_all[:, ln], eg_scr.shape[2:])
        q = {(ci, hk): l2n(conv_silu(ci, hk)) * (GDN_DIM ** -0.5) for ci, hk in hks}
        k = {(ci, hk): l2n(conv_silu(ci, GDN_QK_HEADS + hk)) for ci, hk in hks}
        gram = {x: lax.dot_general(jnp.concatenate([bf(k[x]), bf(q[x])], axis=0),
                                   jnp.concatenate([bf(k[x]), bf(k[x])], axis=0),
                                   NT_DIMS, preferred_element_type=F32) for x in hks}
        yield
        pick = lambda m, hk: jnp.where(first[0:1, :], m[rep * hk:rep * hk + 1, :],
                                       m[rep * hk + 1:rep * hk + 2, :])
        a = {}
        for ci, hk in hks:
            gcol2 = jnp.where(first, gcol[ci, rep * hk], gcol[ci, rep * hk + 1])
            dec = jnp.where(causal, jnp.exp(gcol2 - pick(gc_row[ci][GDN_V_HEADS:], hk)), 0.0)
            brow = pick(beta_row[ci], hk)
            a[ci, hk] = jnp.where(strict, gram[ci, hk][:c] * dec, 0.0) * brow
            intra_scr[wslot, fk(ci, hk)] = bf(gram[ci, hk][c:] * dec * brow)
        n = {x: -a[x] for x in hks}
        p = {x: _dot(bf(a[x]), blockdiag(a[x])) for x in hks}
        yield
        for _ in range(4):
            r_ = {x: _dot(jnp.concatenate([bf(n[x]), bf(p[x])], axis=0), blockdiag(p[x])) for x in hks}
            yield
            n = {x: n[x] + p[x] + r_[x][:c] for x in hks}
            p = {x: r_[x][c:] for x in hks}
        n = {x: n[x] + p[x] + _dot(bf(n[x]), blockdiag(p[x])) for x in hks}
        yield
        rhs = {(ci, h): jnp.concatenate([conv_silu(ci, 2 * GDN_QK_HEADS + h),
                                         k[ci, h // rep] * egc[ci, h]], axis=1) for ci, h in hs}
        zr = jnp.zeros((c, 2 * GDN_DIM), BF16)
        for ci, hk in hks:
            a_, b_ = rep * hk, rep * hk + 1
            rbd = jnp.concatenate([jnp.concatenate([bf(rhs[ci, a_]), zr], axis=1),
                                   jnp.concatenate([zr, bf(rhs[ci, b_])], axis=1)], axis=0)
            uw = (jnp.concatenate([rhs[ci, a_], rhs[ci, b_]], axis=1)
                  + _dot(bf(n[ci, hk]), rbd))
            for i_, h in enumerate((a_, b_)):
                o0 = 2 * GDN_DIM * i_
                u_scr[wslot, fh(ci, h)] = uw[:, o0:o0 + GDN_DIM]
                wq_scr[wslot, fh(ci, h)] = jnp.concatenate(
                    [bf(uw[:, o0 + GDN_DIM:o0 + 2 * GDN_DIM]), bf(q[ci, hk] * egc[ci, h])], axis=0)
                kd_scr[wslot, fh(ci, h)] = bf(k[ci, hk] * kdf[ci, h])

    last = pl.num_programs(1) - 1

    @pl.when(step == 0)
    def _():
        load_block(True)
        _interleave([prep([ci]) for ci in cis])

    @pl.when((step > 0) & (step < last))
    def _():
        load_block(False)
        _interleave([recurrence()] + [prep([ci]) for ci in cis])

    @pl.when(step == last)
    def _():
        _interleave([recurrence()])


def _gdn(proj3d, ba3d, conv_w, prow, onorm, *, qkv_col, z_col):
    b, t, _ = proj3d.shape
    c = CHUNK
    rows = c * GDN_CPB
    nt = t // rows
    conv_ch = conv_w.shape[1]
    v_width = GDN_V_HEADS * GDN_DIM
    cur = lambda bi, si: (bi, jnp.minimum(si, nt - 1))
    prev = lambda bi, si: (bi, jnp.maximum(si - 1, 0))
    const = lambda bi, si: (0, 0)
    heads = (2, GDN_CPB * GDN_V_HEADS)
    return pl.pallas_call(
        _gdn_body,
        grid=(b, nt + 1),
        in_specs=[
            *[pl.BlockSpec((None, rows, GDN_IN_BLOCK), lambda bi, si, j=j: cur(bi, si) + (qkv_col + j,))
              for j in range(conv_ch // GDN_IN_BLOCK)],
            *[pl.BlockSpec((None, rows, GDN_IN_BLOCK), lambda bi, si, j=j: prev(bi, si) + (z_col + j,))
              for j in range(v_width // GDN_IN_BLOCK)],
            pl.BlockSpec((None, rows, LANES), lambda bi, si: cur(bi, si) + (0,)),
            pl.BlockSpec((CONV_WIDTH, conv_ch), const),
            pl.BlockSpec((SUBLANES, LANES), const),
            pl.BlockSpec((1, GDN_DIM), const),
        ],
        out_specs=pl.BlockSpec((None, rows, v_width), lambda bi, si: prev(bi, si) + (0,)),
        out_shape=jax.ShapeDtypeStruct((b, t, v_width), BF16),
        scratch_shapes=[pltpu.VMEM((conv_ch // LANES, rows + SUBLANES, LANES), F32),
                        pltpu.VMEM((GDN_V_HEADS, GDN_DIM, GDN_DIM), F32),
                        pltpu.VMEM(heads + (c, GDN_DIM), F32),
                        pltpu.VMEM(heads + (2 * c, GDN_DIM), BF16),
                        pltpu.VMEM(heads + (c, GDN_DIM), BF16),
                        pltpu.VMEM((2, GDN_CPB * GDN_QK_HEADS, c, 2 * c), BF16),
                        pltpu.VMEM(heads + (SUBLANES, LANES), F32),
                        pltpu.VMEM(heads[1:] + (c, GDN_DIM), F32)],
        compiler_params=pltpu.CompilerParams(
            dimension_semantics=("parallel", "arbitrary"), vmem_limit_bytes=VMEM_LIMIT),
        name="gdn",
    )(*([proj3d] * ((conv_ch + v_width) // GDN_IN_BLOCK)), ba3d, conv_w, prow, onorm)


def _post_body(x_ref, oa_ref, ob_ref, ga_ref, gb_ref, p_ref, wa_ref, wb_ref, wo_ref, g_mix_ref,
               g_pre_ref, wup_ref, wdn_ref, g_post_ref, wple_ref, wgate_ref, g_ple_ref, out_ref,
               *, ff_chunk):
    ya = _dot(oa_ref[...], wa_ref[...])
    yb = _dot(ob_ref[...], wb_ref[...])
    merged = (jax.nn.sigmoid(ga_ref[...].astype(F32)) * ya
              + jax.nn.sigmoid(gb_ref[...].astype(F32)) * yb)
    h = x_ref[...] + _rms(_dot(merged.astype(BF16), wo_ref[...]), g_mix_ref[...])
    u = _rms(h, g_pre_ref[...]).astype(BF16)
    d_ff = wup_ref.shape[1]
    acc = jnp.zeros(h.shape, F32)
    for c0 in range(0, d_ff, ff_chunk):
        hid = jnp.square(jnp.maximum(_dot(u, wup_ref[:, c0:c0 + ff_chunk]), 0.0))
        acc = acc + _dot(hid.astype(BF16), wdn_ref[c0:c0 + ff_chunk, :])
    h = h + _rms(acc, g_post_ref[...])
    e = _dot(p_ref[...].astype(BF16), wple_ref[...]) * jax.nn.sigmoid(
        _dot(h.astype(BF16), wgate_ref[...]))
    out_ref[...] = h + _rms(e, g_ple_ref[...])


def _post(x2d, o_a, o_b, proj, p2d, w_a, w_b, w_o, g_mix, g_pre, w_up, w_dn, g_post, w_ple, w_gate,
          g_ple, *, ga_col, gb_col, tm=512, ff_chunk=1024):
    n, d = x2d.shape
    row = lambda i: (i, 0)
    const = lambda i: (0, 0)
    weight = lambda w: pl.BlockSpec(w.shape, const, pipeline_mode=pl.Buffered(1))
    gain = pl.BlockSpec((1, d), const)
    return pl.pallas_call(
        functools.partial(_post_body, ff_chunk=ff_chunk),
        grid=(n // tm,),
        in_specs=[
            pl.BlockSpec((tm, d), row),
            pl.BlockSpec((tm, o_a.shape[1]), row),
            pl.BlockSpec((tm, o_b.shape[1]), row),
            pl.BlockSpec((tm, d), lambda i: (i, ga_col)),
            pl.BlockSpec((tm, d), lambda i: (i, gb_col)),
            pl.BlockSpec((tm, p2d.shape[1]), row),
            weight(w_a), weight(w_b), weight(w_o), gain,
            gain, weight(w_up), weight(w_dn), gain,
            weight(w_ple), weight(w_gate), gain,
        ],
        out_specs=pl.BlockSpec((tm, d), row),
        out_shape=jax.ShapeDtypeStruct((n, d), F32),
        compiler_params=pltpu.CompilerParams(
            dimension_semantics=("parallel",), vmem_limit_bytes=VMEM_LIMIT),
        name="post",
    )(x2d, o_a, o_b, proj, proj, p2d, w_a, w_b, w_o, g_mix, g_pre, w_up, w_dn, g_post, w_ple,
      w_gate, g_ple)


def _layer(h, p_i, i, pre_mix_norm, w_in, conv_w, lambda_q1, lambda_k1, lambda_q2, lambda_k2,
           da_sub_norm, gdn_a_log, gdn_dt_bias, gdn_out_norm, w_branch_a, w_branch_b, w_out,
           post_mix_norm, pre_mlp_norm, w_up, w_down, post_mlp_norm, w_ple, w_ple_gate, ple_norm):
    b, t, d = h.shape
    n = b * t
    da_w = DA_HEADS * 2 * DA_HEAD_DIM
    gqk_w = GDN_QK_HEADS * GDN_DIM
    gv_w = GDN_V_HEADS * GDN_DIM
    conv_ch = 2 * gqk_w + gv_w
    sizes = (da_w, da_w, da_w, conv_ch, gv_w, GDN_V_HEADS, GDN_V_HEADS, d, d)
    offs = [0]
    for s in sizes:
        offs.append(offs[-1] + s)
    row_scale = jnp.ones((offs[-1], 1), F32).at[:da_w].set(math.log2(math.e))
    w_t = (w_in.T * row_scale).astype(BF16)
    n_head = offs[5]
    w_gate_t = w_t[offs[7]:]
    w_ba_t = jnp.pad(w_t[offs[5]:offs[7]], ((0, LANES - 2 * GDN_V_HEADS), (0, 0)))
    col = {"q": 0, "k": da_w, "v": 2 * da_w, "conv": 3 * da_w, "z": 3 * da_w + conv_ch,
           "ga": n_head, "gb": n_head + d}

    x2d = h.reshape(n, d)
    proj, ba = _in_proj(x2d, pre_mix_norm.reshape(1, d), w_t, w_gate_t, w_ba_t, n_head=n_head)
    proj3d = proj.reshape(b, t, -1)

    lambda_init = 0.8 - 0.6 * math.exp(-0.3 * i)
    o_a = _diff_attn(proj3d, lambda_q1.reshape(1, -1), lambda_k1.reshape(1, -1),
                     lambda_q2.reshape(1, -1), lambda_k2.reshape(1, -1), da_sub_norm.reshape(1, -1),
                     q_col=col["q"] // LANES, k_col=col["k"] // LANES, v_col=col["v"] // LANES,
                     lambda_init=lambda_init)

    lanes = jnp.zeros((LANES,), F32)
    a_log = lanes.at[GDN_V_HEADS:2 * GDN_V_HEADS].set(gdn_a_log.astype(F32))
    dt_b = lanes.at[GDN_V_HEADS:2 * GDN_V_HEADS].set(gdn_dt_bias.astype(F32))
    prow = jnp.zeros((SUBLANES, LANES), F32).at[0].set(a_log).at[1].set(dt_b)
    o_b = _gdn(proj3d, ba.reshape(b, t, LANES), conv_w, prow, gdn_out_norm.reshape(1, -1),
               qkv_col=col["conv"] // GDN_IN_BLOCK, z_col=col["z"] // GDN_IN_BLOCK)

    out = _post(x2d, o_a.reshape(n, -1), o_b.reshape(n, -1), proj, p_i.reshape(n, -1),
                w_branch_a.astype(BF16), w_branch_b.astype(BF16), w_out.astype(BF16),
                post_mix_norm.reshape(1, d), pre_mlp_norm.reshape(1, d), w_up.astype(BF16),
                w_down.astype(BF16), post_mlp_norm.reshape(1, d), w_ple.astype(BF16),
                w_ple_gate.astype(BF16), ple_norm.reshape(1, d),
                ga_col=col["ga"] // d, gb_col=col["gb"] // d)
    return out.reshape(b, t, d)


def kernel(x, p, pre_mix_norm, w_in, conv_w, lambda_q1, lambda_k1, lambda_q2, lambda_k2,
           da_sub_norm, gdn_a_log, gdn_dt_bias, gdn_out_norm, w_branch_a, w_branch_b, w_out,
           post_mix_norm, pre_mlp_norm, w_up, w_down, post_mlp_norm, w_ple, w_ple_gate, ple_norm):
    per_layer = (pre_mix_norm, w_in, conv_w, lambda_q1, lambda_k1, lambda_q2, lambda_k2,
                 da_sub_norm, gdn_a_log, gdn_dt_bias, gdn_out_norm, w_branch_a, w_branch_b, w_out,
                 post_mix_norm, pre_mlp_norm, w_up, w_down, post_mlp_norm, w_ple, w_ple_gate,
                 ple_norm)
    h = x
    for i in range(p.shape[0]):
        h = _layer(h, p[i], i, *(w[i] for w in per_layer))
    return h
```

```python
import functools
import math

import jax
import jax.numpy as jnp
from jax import lax
from jax.experimental import pallas as pl
from jax.experimental.pallas import tpu as pltpu

F32 = jnp.float32
BF16 = jnp.bfloat16

EPS = 1e-6
LANES = 128
SUBLANES = 8
BF16_ROWS = 16
DA_HEADS = 8
DA_HEAD_DIM = 64
GDN_QK_HEADS = 8
GDN_V_HEADS = 16
GDN_DIM = 128
CONV_WIDTH = 4
CHUNK = 64
GDN_CPB = 4
ROW_STRIDE = 4
ROW_GROUP = ROW_STRIDE * SUBLANES
GDN_IN_BLOCK = 1024
ATTN_CHAINS = 4
ATTN_HEADS_PER_STEP = 2
NEG = -1e30
VMEM_LIMIT = 56 * 1024 * 1024

NT_DIMS = (((1,), (1,)), ((), ()))
TN_DIMS = (((0,), (0,)), ((), ()))


def _rms(x, gain):
    return x * lax.rsqrt(jnp.mean(x * x, axis=-1, keepdims=True) + EPS) * gain


def _dot(a, b):
    return jnp.dot(a, b, preferred_element_type=F32)


def _inproj_body(x_ref, gain_ref, w_ref, wg_ref, wba_ref, out_ref, ba_ref, u_scr, *, n_head_blocks):
    j = pl.program_id(1)

    @pl.when(j == 0)
    def _():
        u = _rms(x_ref[...], gain_ref[...]).astype(BF16)
        u_scr[...] = u
        ba_ref[...] = lax.dot_general(u, wba_ref[...], NT_DIMS, preferred_element_type=F32)

    def project(w):
        out_ref[...] = lax.dot_general(u_scr[...], w[...], NT_DIMS,
                                       preferred_element_type=F32).astype(out_ref.dtype)

    pl.when(j < n_head_blocks)(lambda: project(w_ref))
    pl.when(j >= n_head_blocks)(lambda: project(wg_ref))


def _in_proj(x2d, gain, w_t, w_gate_t, w_ba_t, *, n_head, tm=2048, tn=1024):
    n, d = x2d.shape
    n_head_blocks = n_head // tn
    width = n_head + w_gate_t.shape[0]
    return pl.pallas_call(
        functools.partial(_inproj_body, n_head_blocks=n_head_blocks),
        grid=(n // tm, width // tn),
        in_specs=[
            pl.BlockSpec((tm, d), lambda i, j: (i, 0)),
            pl.BlockSpec((1, d), lambda i, j: (0, 0)),
            pl.BlockSpec((tn, d), lambda i, j: (jnp.minimum(j, n_head_blocks - 1), 0)),
            pl.BlockSpec((tn, d), lambda i, j: (jnp.maximum(j - n_head_blocks, 0), 0)),
            pl.BlockSpec((LANES, d), lambda i, j: (0, 0)),
        ],
        out_specs=[
            pl.BlockSpec((tm, tn), lambda i, j: (i, j)),
            pl.BlockSpec((tm, LANES), lambda i, j: (i, 0)),
        ],
        out_shape=[
            jax.ShapeDtypeStruct((n, width), BF16),
            jax.ShapeDtypeStruct((n, LANES), F32),
        ],
        scratch_shapes=[pltpu.VMEM((tm, d), BF16)],
        compiler_params=pltpu.CompilerParams(
            dimension_semantics=("parallel", "arbitrary"), vmem_limit_bytes=VMEM_LIMIT),
        name="in_proj",
    )(x2d, gain, w_t, w_gate_t, w_ba_t)


def _interleave(chains):
    chains = list(chains)
    while chains:
        for g in list(chains):
            if next(g, StopIteration) is StopIteration:
                chains.remove(g)


def _attn_body(lq1_ref, lk1_ref, lq2_ref, lk2_ref, gain_ref, q_ref, k_ref, v_ref, o_ref,
               qt_scr, vt_scr, *, tq, lambda_init):
    t = q_ref.shape[0]
    nq = t // tq
    heads = q_ref.shape[1] // LANES
    scale = jnp.asarray(DA_HEAD_DIM ** -0.5, BF16)
    for hd in range(heads):
        hl = slice(hd * LANES, (hd + 1) * LANES)
        for i in range(nq):
            sl = slice(i * tq, (i + 1) * tq)
            qt_scr[hd, :, sl] = (q_ref[sl, hl] * scale).T
            vt_scr[hd, 0:LANES, sl] = v_ref[sl, hl].T
        vt_scr[hd, LANES:, :] = jnp.ones((vt_scr.shape[1] - LANES, t), BF16)
    lam = (jnp.exp(jnp.sum(lq1_ref[...] * lk1_ref[...], axis=-1, keepdims=True))
           - jnp.exp(jnp.sum(lq2_ref[...] * lk2_ref[...], axis=-1, keepdims=True)) + lambda_init)
    feat = lax.broadcasted_iota(jnp.int32, (LANES, tq), 0)
    kv_row = lax.broadcasted_iota(jnp.int32, (tq, 2 * tq), 0)
    q_col = lax.broadcasted_iota(jnp.int32, (tq, 2 * tq), 1)
    on_or_below_diag = kv_row <= jnp.where(q_col >= tq, q_col - tq, q_col)

    def chain(hd, tiles):
        hl = slice(hd * LANES, (hd + 1) * LANES)
        for qi in tiles:
            qsl = slice(qi * tq, (qi + 1) * tq)
            qt = qt_scr[hd, :, qsl]
            zero = jnp.zeros_like(qt)
            qs = jnp.concatenate([jnp.where(feat < DA_HEAD_DIM, qt, zero),
                                  jnp.where(feat >= DA_HEAD_DIM, qt, zero)], axis=1)
            scores = lambda j: _dot(k_ref[j * tq:(j + 1) * tq, hl], qs)
            m = jnp.full((1, 2 * tq), NEG, F32)
            acc = jnp.zeros((vt_scr.shape[1], 2 * tq), F32)
            s_next = scores(0)
            yield
            for j in range(qi + 1):
                s = s_next
                if j < qi:
                    s_next = scores(j + 1)
                    yield
                else:
                    s = jnp.where(on_or_below_diag, s, NEG)
                m_new = jnp.maximum(m, jnp.max(s, axis=0, keepdims=True))
                alpha = jnp.exp2(m - m_new)
                p = jnp.exp2(s - m_new)
                pv = _dot(vt_scr[hd, :, j * tq:(j + 1) * tq], p.astype(BF16))
                yield
                acc = alpha * acc + pv
                m = m_new
            on = acc[0:LANES] * (1.0 / acc[LANES:LANES + 1])
            o = on[:, :tq] - lam * on[:, tq:]
            y = o * lax.rsqrt(jnp.mean(o * o, axis=0, keepdims=True) + EPS)
            o_ref[qsl, hl] = (y.T * gain_ref[...] * (1.0 - lambda_init)).astype(o_ref.dtype)

    pairs = [(hd, [nq - 1 - i, i]) for hd in range(heads) for i in range(nq // 2)]
    for c0 in range(0, len(pairs), ATTN_CHAINS):
        _interleave(chain(hd, tiles) for hd, tiles in pairs[c0:c0 + ATTN_CHAINS])


def _diff_attn(proj3d, lq1, lk1, lq2, lk2, sub_gain, *, q_col, k_col, v_col, lambda_init, tq=256):
    b, t, _ = proj3d.shape
    hps = ATTN_HEADS_PER_STEP
    width = hps * LANES
    vec = pl.BlockSpec((1, DA_HEAD_DIM), lambda bi, h: (0, 0))
    head = lambda col: pl.BlockSpec((None, t, width), lambda bi, h: (bi, 0, col // hps + h))
    return pl.pallas_call(
        functools.partial(_attn_body, tq=tq, lambda_init=lambda_init),
        grid=(b, DA_HEADS // hps),
        in_specs=[vec, vec, vec, vec, pl.BlockSpec((1, LANES), lambda bi, h: (0, 0)),
                  head(q_col), head(k_col), head(v_col)],
        out_specs=pl.BlockSpec((None, t, width), lambda bi, h: (bi, 0, h)),
        out_shape=jax.ShapeDtypeStruct((b, t, DA_HEADS * LANES), BF16),
        scratch_shapes=[pltpu.VMEM((hps, LANES, t), BF16),
                        pltpu.VMEM((hps, LANES + BF16_ROWS, t), BF16)],
        compiler_params=pltpu.CompilerParams(
            dimension_semantics=("parallel", "parallel"), vmem_limit_bytes=VMEM_LIMIT),
        name="diff_attn",
    )(lq1, lk1, lq2, lk2, sub_gain, proj3d, proj3d, proj3d)


def _softplus(x):
    return jnp.maximum(x, 0.0) + jnp.log1p(jnp.exp(-jnp.abs(x)))


def _gdn_body(*refs):
    n_qkv = (2 * GDN_QK_HEADS + GDN_V_HEADS) * GDN_DIM // GDN_IN_BLOCK
    n_z = GDN_V_HEADS * GDN_DIM // GDN_IN_BLOCK
    qkv_refs, z_refs = refs[:n_qkv], refs[n_qkv:n_qkv + n_z]
    (ba_ref, convw_ref, prow_ref, onorm_ref, o_ref,
     xx_scr, state_scr, u_scr, wq_scr, kd_scr, intra_scr, eg_scr, out_scr) = refs[n_qkv + n_z:]
    slabs_per_block = GDN_IN_BLOCK // LANES
    c = CHUNK
    rows = ba_ref.shape[0]
    cis = list(range(rows // c))
    step = pl.program_id(1)
    wslot = step % 2
    rslot = 1 - wslot
    rep = GDN_V_HEADS // GDN_QK_HEADS
    hks = [(ci, hk) for ci in cis for hk in range(GDN_QK_HEADS)]
    hs = [(ci, h) for ci in cis for h in range(GDN_V_HEADS)]
    fh = lambda ci, h: ci * GDN_V_HEADS + h
    fk = lambda ci, hk: ci * GDN_QK_HEADS + hk
    bf = lambda y: y.astype(BF16)

    halo = SUBLANES

    def load_block(first_block):
        if first_block:
            xx_scr[:, 0:halo, :] = jnp.zeros((xx_scr.shape[0], halo, LANES), F32)
            state_scr[...] = jnp.zeros(state_scr.shape, F32)
        else:
            xx_scr[:, 0:halo, :] = xx_scr[:, rows:rows + halo, :]
        for slab in range(xx_scr.shape[0]):
            blk, off = divmod(slab, slabs_per_block)
            xx_scr[slab, halo:halo + rows, :] = (
                qkv_refs[blk][:, off * LANES:(off + 1) * LANES].astype(F32))

    def row_blocks():
        return [ROW_GROUP * g + r for g in range(c // ROW_GROUP) for r in range(ROW_STRIDE)]

    def perm_time(i):
        sub_bits, stride_bits = SUBLANES.bit_length() - 1, ROW_STRIDE.bit_length() - 1
        group = i >> (sub_bits + stride_bits)
        return ((group << (sub_bits + stride_bits)) + ((i & (SUBLANES - 1)) << stride_bits)
                + ((i >> sub_bits) & (ROW_STRIDE - 1)))

    def conv_silu(ci, slab):
        sl = slice(slab * LANES, (slab + 1) * LANES)
        w = [convw_ref[k:k + 1, sl] for k in range(CONV_WIDTH)]
        outs = []
        for g in range(c // ROW_GROUP):
            z = {rho: xx_scr[slab, pl.ds(halo + c * ci + ROW_GROUP * g + rho, SUBLANES,
                                         stride=ROW_STRIDE), :]
                 for rho in range(1 - CONV_WIDTH, ROW_STRIDE)}
            for r in range(ROW_STRIDE):
                y = w[CONV_WIDTH - 1] * z[r]
                for d in range(1, CONV_WIDTH):
                    y = y + w[CONV_WIDTH - 1 - d] * z[r - d]
                outs.append(y)
        y = jnp.concatenate(outs, axis=0)
        return y * jax.nn.sigmoid(y)

    def l2n(y):
        return y * lax.rsqrt(jnp.sum(y * y, axis=-1, keepdims=True) + EPS)

    lane = lax.broadcasted_iota(jnp.int32, (c, 2 * c), 1)
    first = lane < c
    zeros_c = jnp.zeros((c, 2 * c), BF16)

    def blockdiag(x):
        xb = bf(x)
        return jnp.concatenate([jnp.where(first, xb, zeros_c), jnp.where(first, zeros_c, xb)], axis=0)

    def recurrence():
        s = {h: state_scr[h] for h in range(GDN_V_HEADS)}
        o = {}
        for ci in cis:
            heads = range(GDN_V_HEADS)
            ws = {h: _dot(wq_scr[rslot, fh(ci, h)], bf(s[h])) for h in heads}
            yield
            vnb = {h: bf(u_scr[rslot, fh(ci, h)] - ws[h][:c]) for h in heads}
            zv = jnp.zeros((c, GDN_DIM), BF16)
            for hk in range(GDN_QK_HEADS):
                a_, b_ = rep * hk, rep * hk + 1
                vbd = jnp.concatenate([jnp.concatenate([vnb[a_], zv], axis=1),
                                       jnp.concatenate([zv, vnb[b_]], axis=1)], axis=0)
                ov = _dot(intra_scr[rslot, fk(ci, hk)], vbd)
                o[ci, a_] = ws[a_][c:] + ov[:, :GDN_DIM]
                o[ci, b_] = ws[b_][c:] + ov[:, GDN_DIM:]
            s = {h: s[h] * eg_scr[rslot, fh(ci, h)][0:1, :] + lax.dot_general(
                kd_scr[rslot, fh(ci, h)], vnb[h], TN_DIMS, preferred_element_type=F32) for h in heads}
            yield
        for h in range(GDN_V_HEADS):
            state_scr[h] = s[h]
        for ci, h in hs:
            y = _rms(o[ci, h], onorm_ref[...])
            for m, row0 in enumerate(row_blocks()):
                out_scr[fh(ci, h), pl.ds(row0, SUBLANES, stride=ROW_STRIDE), :] = (
                    y[SUBLANES * m:SUBLANES * (m + 1)])
        for ci, h in hs:
            blk, off = divmod(h, slabs_per_block)
            z = z_refs[blk][c * ci:c * (ci + 1), off * LANES:(off + 1) * LANES].astype(F32)
            o_ref[c * ci:c * (ci + 1), h * GDN_DIM:(h + 1) * GDN_DIM] = (
                out_scr[fh(ci, h)] * (z * jax.nn.sigmoid(z))).astype(o_ref.dtype)

    def prep(cis):
        hks = [(ci, hk) for ci in cis for hk in range(GDN_QK_HEADS)]
        hs = [(ci, h) for ci in cis for h in range(GDN_V_HEADS)]
        ii = perm_time(lax.broadcasted_iota(jnp.int32, (c, c), 0))
        jj = perm_time(lax.broadcasted_iota(jnp.int32, (c, c), 1))
        row2 = perm_time(lax.broadcasted_iota(jnp.int32, (c, 2 * c), 0))
        col2 = perm_time(jnp.where(first, lane, lane - c))
        strict = row2 > col2
        causal = row2 >= col2
        hp = lax.Precision.HIGHEST
        dup = lambda m: jnp.concatenate([m, m], axis=1)
        tril = jnp.where(ii >= jj, 1.0, 0.0).astype(F32)
        triu2 = dup(jnp.where(ii <= jj, 1.0, 0.0).astype(F32))
        eye2 = dup(jnp.where(ii == jj, 1.0, 0.0).astype(F32))
        gc_row, beta_row, gcol, egc, kdf = {}, {}, {}, {}, {}
        for ci in cis:
            ba = jnp.concatenate([ba_ref[pl.ds(c * ci + row0, SUBLANES, stride=ROW_STRIDE), :]
                                  for row0 in row_blocks()], axis=0)
            beta_col = jax.nn.sigmoid(ba)
            g_col = -jnp.exp(prow_ref[0:1, :]) * _softplus(ba + prow_ref[1:2, :])
            gc_col = jnp.dot(tril, g_col, precision=hp, preferred_element_type=F32)
            gc_row[ci] = lax.dot_general(g_col, triu2, TN_DIMS, precision=hp,
                                         preferred_element_type=F32)
            beta_row[ci] = lax.dot_general(beta_col, eye2, TN_DIMS, precision=hp,
                                           preferred_element_type=F32)
            g_last = gc_col[c - 1:c, :]
            egc_all = jnp.exp(gc_col)
            kdf_all = pltpu.roll(beta_col, GDN_V_HEADS, axis=1) * jnp.exp(g_last - gc_col)
            eg_all = jnp.exp(g_last)
            for h in range(GDN_V_HEADS):
                ln = slice(GDN_V_HEADS + h, GDN_V_HEADS + h + 1)
                gcol[ci, h] = jnp.broadcast_to(gc_col[:, ln], (c, 2 * c))
                egc[ci, h] = jnp.broadcast_to(egc_all[:, ln], (c, GDN_DIM))
                kdf[ci, h] = jnp.broadcast_to(kdf_all[:, ln], (c, GDN_DIM))
                eg_scr[wslot, fh(ci, h)] = jnp.broadcast_to(eg_all[:, ln], eg_scr.shape[2:])
        q = {(ci, hk): l2n(conv_silu(ci, hk)) * (GDN_DIM ** -0.5) for ci, hk in hks}
        k = {(ci, hk): l2n(conv_silu(ci, GDN_QK_HEADS + hk)) for ci, hk in hks}
        gram = {x: lax.dot_general(jnp.concatenate([bf(k[x]), bf(q[x])], axis=0),
                                   jnp.concatenate([bf(k[x]), bf(k[x])], axis=0),
                                   NT_DIMS, preferred_element_type=F32) for x in hks}
        yield
        pick = lambda m, hk: jnp.where(first[0:1, :], m[rep * hk:rep * hk + 1, :],
                                       m[rep * hk + 1:rep * hk + 2, :])
        a = {}
        for ci, hk in hks:
            gcol2 = jnp.where(first, gcol[ci, rep * hk], gcol[ci, rep * hk + 1])
            dec = jnp.where(causal, jnp.exp(gcol2 - pick(gc_row[ci][GDN_V_HEADS:], hk)), 0.0)
            brow = pick(beta_row[ci], hk)
            a[ci, hk] = jnp.where(strict, gram[ci, hk][:c] * dec, 0.0) * brow
            intra_scr[wslot, fk(ci, hk)] = bf(gram[ci, hk][c:] * dec * brow)
        n = {x: -a[x] for x in hks}
        p = {x: _dot(bf(a[x]), blockdiag(a[x])) for x in hks}
        yield
        for _ in range(4):
            r_ = {x: _dot(jnp.concatenate([bf(n[x]), bf(p[x])], axis=0), blockdiag(p[x])) for x in hks}
            yield
            n = {x: n[x] + p[x] + r_[x][:c] for x in hks}
            p = {x: r_[x][c:] for x in hks}
        n = {x: n[x] + p[x] + _dot(bf(n[x]), blockdiag(p[x])) for x in hks}
        yield
        rhs = {(ci, h): jnp.concatenate([conv_silu(ci, 2 * GDN_QK_HEADS + h),
                                         k[ci, h // rep] * egc[ci, h]], axis=1) for ci, h in hs}
        zr = jnp.zeros((c, 2 * GDN_DIM), BF16)
        for ci, hk in hks:
            a_, b_ = rep * hk, rep * hk + 1
            rbd = jnp.concatenate([jnp.concatenate([bf(rhs[ci, a_]), zr], axis=1),
                                   jnp.concatenate([zr, bf(rhs[ci, b_])], axis=1)], axis=0)
            uw = (jnp.concatenate([rhs[ci, a_], rhs[ci, b_]], axis=1)
                  + _dot(bf(n[ci, hk]), rbd))
            for i_, h in enumerate((a_, b_)):
                o0 = 2 * GDN_DIM * i_
                u_scr[wslot, fh(ci, h)] = uw[:, o0:o0 + GDN_DIM]
                wq_scr[wslot, fh(ci, h)] = jnp.concatenate(
                    [bf(uw[:, o0 + GDN_DIM:o0 + 2 * GDN_DIM]), bf(q[ci, hk] * egc[ci, h])], axis=0)
                kd_scr[wslot, fh(ci, h)] = bf(k[ci, hk] * kdf[ci, h])

    last = pl.num_programs(1) - 1

    @pl.when(step == 0)
    def _():
        load_block(True)
        _interleave([prep([ci]) for ci in cis])

    @pl.when((step > 0) & (step < last))
    def _():
        load_block(False)
        _interleave([recurrence()] + [prep([ci]) for ci in cis])

    @pl.when(step == last)
    def _():
        _interleave([recurrence()])


def _gdn(proj3d, ba3d, conv_w, prow, onorm, *, qkv_col, z_col):
    b, t, _ = proj3d.shape
    c = CHUNK
    rows = c * GDN_CPB
    nt = t // rows
    conv_ch = conv_w.shape[1]
    v_width = GDN_V_HEADS * GDN_DIM
    cur = lambda bi, si: (bi, jnp.minimum(si, nt - 1))
    prev = lambda bi, si: (bi, jnp.maximum(si - 1, 0))
    const = lambda bi, si: (0, 0)
    heads = (2, GDN_CPB * GDN_V_HEADS)
    return pl.pallas_call(
        _gdn_body,
        grid=(b, nt + 1),
        in_specs=[
            *[pl.BlockSpec((None, rows, GDN_IN_BLOCK), lambda bi, si, j=j: cur(bi, si) + (qkv_col + j,))
              for j in range(conv_ch // GDN_IN_BLOCK)],
            *[pl.BlockSpec((None, rows, GDN_IN_BLOCK), lambda bi, si, j=j: prev(bi, si) + (z_col + j,))
              for j in range(v_width // GDN_IN_BLOCK)],
            pl.BlockSpec((None, rows, LANES), lambda bi, si: cur(bi, si) + (0,)),
            pl.BlockSpec((CONV_WIDTH, conv_ch), const),
            pl.BlockSpec((SUBLANES, LANES), const),
            pl.BlockSpec((1, GDN_DIM), const),
        ],
        out_specs=pl.BlockSpec((None, rows, v_width), lambda bi, si: prev(bi, si) + (0,)),
        out_shape=jax.ShapeDtypeStruct((b, t, v_width), BF16),
        scratch_shapes=[pltpu.VMEM((conv_ch // LANES, rows + SUBLANES, LANES), F32),
                        pltpu.VMEM((GDN_V_HEADS, GDN_DIM, GDN_DIM), F32),
                        pltpu.VMEM(heads + (c, GDN_DIM), F32),
                        pltpu.VMEM(heads + (2 * c, GDN_DIM), BF16),
                        pltpu.VMEM(heads + (c, GDN_DIM), BF16),
                        pltpu.VMEM((2, GDN_CPB * GDN_QK_HEADS, c, 2 * c), BF16),
                        pltpu.VMEM(heads + (SUBLANES, LANES), F32),
                        pltpu.VMEM(heads[1:] + (c, GDN_DIM), F32)],
        compiler_params=pltpu.CompilerParams(
            dimension_semantics=("parallel", "arbitrary"), vmem_limit_bytes=VMEM_LIMIT),
        name="gdn",
    )(*([proj3d] * ((conv_ch + v_width) // GDN_IN_BLOCK)), ba3d, conv_w, prow, onorm)


def _post_body(x_ref, oa_ref, ob_ref, ga_ref, gb_ref, p_ref, wa_ref, wb_ref, wo_ref, g_mix_ref,
               g_pre_ref, wup_ref, wdn_ref, g_post_ref, wple_ref, wgate_ref, g_ple_ref, out_ref,
               *, ff_chunk):
    ya = _dot(oa_ref[...], wa_ref[...])
    yb = _dot(ob_ref[...], wb_ref[...])
    merged = (jax.nn.sigmoid(ga_ref[...].astype(F32)) * ya
              + jax.nn.sigmoid(gb_ref[...].astype(F32)) * yb)
    h = x_ref[...] + _rms(_dot(merged.astype(BF16), wo_ref[...]), g_mix_ref[...])
    u = _rms(h, g_pre_ref[...]).astype(BF16)
    d_ff = wup_ref.shape[1]
    acc = jnp.zeros(h.shape, F32)
    for c0 in range(0, d_ff, ff_chunk):
        hid = jnp.square(jnp.maximum(_dot(u, wup_ref[:, c0:c0 + ff_chunk]), 0.0))
        acc = acc + _dot(hid.astype(BF16), wdn_ref[c0:c0 + ff_chunk, :])
    h = h + _rms(acc, g_post_ref[...])
    e = _dot(p_ref[...].astype(BF16), wple_ref[...]) * jax.nn.sigmoid(
        _dot(h.astype(BF16), wgate_ref[...]))
    out_ref[...] = h + _rms(e, g_ple_ref[...])


def _post(x2d, o_a, o_b, proj, p2d, w_a, w_b, w_o, g_mix, g_pre, w_up, w_dn, g_post, w_ple, w_gate,
          g_ple, *, ga_col, gb_col, tm=512, ff_chunk=1024):
    n, d = x2d.shape
    row = lambda i: (i, 0)
    const = lambda i: (0, 0)
    weight = lambda w: pl.BlockSpec(w.shape, const, pipeline_mode=pl.Buffered(1))
    gain = pl.BlockSpec((1, d), const)
    return pl.pallas_call(
        functools.partial(_post_body, ff_chunk=ff_chunk),
        grid=(n // tm,),
        in_specs=[
            pl.BlockSpec((tm, d), row),
            pl.BlockSpec((tm, o_a.shape[1]), row),
            pl.BlockSpec((tm, o_b.shape[1]), row),
            pl.BlockSpec((tm, d), lambda i: (i, ga_col)),
            pl.BlockSpec((tm, d), lambda i: (i, gb_col)),
            pl.BlockSpec((tm, p2d.shape[1]), row),
            weight(w_a), weight(w_b), weight(w_o), gain,
            gain, weight(w_up), weight(w_dn), gain,
            weight(w_ple), weight(w_gate), gain,
        ],
        out_specs=pl.BlockSpec((tm, d), row),
        out_shape=jax.ShapeDtypeStruct((n, d), F32),
        compiler_params=pltpu.CompilerParams(
            dimension_semantics=("parallel",), vmem_limit_bytes=VMEM_LIMIT),
        name="post",
    )(x2d, o_a, o_b, proj, proj, p2d, w_a, w_b, w_o, g_mix, g_pre, w_up, w_dn, g_post, w_ple,
      w_gate, g_ple)


def _layer(h, p_i, i, pre_mix_norm, w_in, conv_w, lambda_q1, lambda_k1, lambda_q2, lambda_k2,
           da_sub_norm, gdn_a_log, gdn_dt_bias, gdn_out_norm, w_branch_a, w_branch_b, w_out,
           post_mix_norm, pre_mlp_norm, w_up, w_down, post_mlp_norm, w_ple, w_ple_gate, ple_norm):
    b, t, d = h.shape
    n = b * t
    da_w = DA_HEADS * 2 * DA_HEAD_DIM
    gqk_w = GDN_QK_HEADS * GDN_DIM
    gv_w = GDN_V_HEADS * GDN_DIM
    conv_ch = 2 * gqk_w + gv_w
    sizes = (da_w, da_w, da_w, conv_ch, gv_w, GDN_V_HEADS, GDN_V_HEADS, d, d)
    offs = [0]
    for s in sizes:
        offs.append(offs[-1] + s)
    row_scale = jnp.ones((offs[-1], 1), F32).at[:da_w].set(math.log2(math.e))
    w_t = (w_in.T * row_scale).astype(BF16)
    n_head = offs[5]
    w_gate_t = w_t[offs[7]:]
    w_ba_t = jnp.pad(w_t[offs[5]:offs[7]], ((0, LANES - 2 * GDN_V_HEADS), (0, 0)))
    col = {"q": 0, "k": da_w, "v": 2 * da_w, "conv": 3 * da_w, "z": 3 * da_w + conv_ch,
           "ga": n_head, "gb": n_head + d}

    x2d = h.reshape(n, d)
    proj, ba = _in_proj(x2d, pre_mix_norm.reshape(1, d), w_t, w_gate_t, w_ba_t, n_head=n_head)
    proj3d = proj.reshape(b, t, -1)

    lambda_init = 0.8 - 0.6 * math.exp(-0.3 * i)
    o_a = _diff_attn(proj3d, lambda_q1.reshape(1, -1), lambda_k1.reshape(1, -1),
                     lambda_q2.reshape(1, -1), lambda_k2.reshape(1, -1), da_sub_norm.reshape(1, -1),
                     q_col=col["q"] // LANES, k_col=col["k"] // LANES, v_col=col["v"] // LANES,
                     lambda_init=lambda_init)

    lanes = jnp.zeros((LANES,), F32)
    a_log = lanes.at[GDN_V_HEADS:2 * GDN_V_HEADS].set(gdn_a_log.astype(F32))
    dt_b = lanes.at[GDN_V_HEADS:2 * GDN_V_HEADS].set(gdn_dt_bias.astype(F32))
    prow = jnp.zeros((SUBLANES, LANES), F32).at[0].set(a_log).at[1].set(dt_b)
    o_b = _gdn(proj3d, ba.reshape(b, t, LANES), conv_w, prow, gdn_out_norm.reshape(1, -1),
               qkv_col=col["conv"] // GDN_IN_BLOCK, z_col=col["z"] // GDN_IN_BLOCK)

    out = _post(x2d, o_a.reshape(n, -1), o_b.reshape(n, -1), proj, p_i.reshape(n, -1),
                w_branch_a.astype(BF16), w_branch_b.astype(BF16), w_out.astype(BF16),
                post_mix_norm.reshape(1, d), pre_mlp_norm.reshape(1, d), w_up.astype(BF16),
                w_down.astype(BF16), post_mlp_norm.reshape(1, d), w_ple.astype(BF16),
                w_ple_gate.astype(BF16), ple_norm.reshape(1, d),
                ga_col=col["ga"] // d, gb_col=col["gb"] // d)
    return out.reshape(b, t, d)


def kernel(x, p, pre_mix_norm, w_in, conv_w, lambda_q1, lambda_k1, lambda_q2, lambda_k2,
           da_sub_norm, gdn_a_log, gdn_dt_bias, gdn_out_norm, w_branch_a, w_branch_b, w_out,
           post_mix_norm, pre_mlp_norm, w_up, w_down, post_mlp_norm, w_ple, w_ple_gate, ple_norm):
    per_layer = (pre_mix_norm, w_in, conv_w, lambda_q1, lambda_k1, lambda_q2, lambda_k2,
                 da_sub_norm, gdn_a_log, gdn_dt_bias, gdn_out_norm, w_branch_a, w_branch_b, w_out,
                 post_mix_norm, pre_mlp_norm, w_up, w_down, post_mlp_norm, w_ple, w_ple_gate,
                 ple_norm)
    h = x
    for i in range(p.shape[0]):
        h = _layer(h, p[i], i, *(w[i] for w in per_layer))
    return h
```

```python
import functools
import math

import jax
import jax.numpy as jnp
from jax import lax
from jax.experimental import pallas as pl
from jax.experimental.pallas import tpu as pltpu

F32 = jnp.float32
BF16 = jnp.bfloat16

EPS = 1e-6
LANES = 128
SUBLANES = 8
BF16_ROWS = 16
DA_HEADS = 8
DA_HEAD_DIM = 64
GDN_QK_HEADS = 8
GDN_V_HEADS = 16
GDN_DIM = 128
CONV_WIDTH = 4
CHUNK = 128
GDN_CPB = 2
ROW_STRIDE = 4
ROW_GROUP = ROW_STRIDE * SUBLANES
GDN_IN_BLOCK = 1024
ATTN_CHAINS = 4
NEG = -1e30
VMEM_LIMIT = 56 * 1024 * 1024

NT_DIMS = (((1,), (1,)), ((), ()))
TN_DIMS = (((0,), (0,)), ((), ()))


def _rms(x, gain):
    return x * lax.rsqrt(jnp.mean(x * x, axis=-1, keepdims=True) + EPS) * gain


def _dot(a, b):
    return jnp.dot(a, b, preferred_element_type=F32)


def _inproj_body(x_ref, gain_ref, w_ref, wg_ref, wba_ref, out_ref, ba_ref, u_scr, *, n_head_blocks):
    j = pl.program_id(1)

    @pl.when(j == 0)
    def _():
        u = _rms(x_ref[...], gain_ref[...]).astype(BF16)
        u_scr[...] = u
        ba_ref[...] = lax.dot_general(u, wba_ref[...], NT_DIMS, preferred_element_type=F32)

    def project(w):
        out_ref[...] = lax.dot_general(u_scr[...], w[...], NT_DIMS,
                                       preferred_element_type=F32).astype(out_ref.dtype)

    pl.when(j < n_head_blocks)(lambda: project(w_ref))
    pl.when(j >= n_head_blocks)(lambda: project(wg_ref))


def _in_proj(x2d, gain, w_t, w_gate_t, w_ba_t, *, n_head, tm=2048, tn=1024):
    n, d = x2d.shape
    n_head_blocks = n_head // tn
    width = n_head + w_gate_t.shape[0]
    return pl.pallas_call(
        functools.partial(_inproj_body, n_head_blocks=n_head_blocks),
        grid=(n // tm, width // tn),
        in_specs=[
            pl.BlockSpec((tm, d), lambda i, j: (i, 0)),
            pl.BlockSpec((1, d), lambda i, j: (0, 0)),
            pl.BlockSpec((tn, d), lambda i, j: (jnp.minimum(j, n_head_blocks - 1), 0)),
            pl.BlockSpec((tn, d), lambda i, j: (jnp.maximum(j - n_head_blocks, 0), 0)),
            pl.BlockSpec((LANES, d), lambda i, j: (0, 0)),
        ],
        out_specs=[
            pl.BlockSpec((tm, tn), lambda i, j: (i, j)),
            pl.BlockSpec((tm, LANES), lambda i, j: (i, 0)),
        ],
        out_shape=[
            jax.ShapeDtypeStruct((n, width), BF16),
            jax.ShapeDtypeStruct((n, LANES), F32),
        ],
        scratch_shapes=[pltpu.VMEM((tm, d), BF16)],
        compiler_params=pltpu.CompilerParams(
            dimension_semantics=("parallel", "arbitrary"), vmem_limit_bytes=VMEM_LIMIT),
        name="in_proj",
    )(x2d, gain, w_t, w_gate_t, w_ba_t)


def _interleave(chains):
    chains = list(chains)
    while chains:
        for g in list(chains):
            if next(g, StopIteration) is StopIteration:
                chains.remove(g)


def _attn_body(lq1_ref, lk1_ref, lq2_ref, lk2_ref, gain_ref, q_ref, k_ref, v_ref, o_ref,
               qt_scr, vt_scr, *, tq, lambda_init):
    t = q_ref.shape[0]
    nq = t // tq
    scale = jnp.asarray(DA_HEAD_DIM ** -0.5, BF16)
    for i in range(nq):
        sl = slice(i * tq, (i + 1) * tq)
        qt_scr[:, sl] = (q_ref[sl, :] * scale).T
        vt_scr[0:LANES, sl] = v_ref[sl, :].T
    vt_scr[LANES:, :] = jnp.ones((vt_scr.shape[0] - LANES, t), BF16)
    lam = (jnp.exp(jnp.sum(lq1_ref[...] * lk1_ref[...], axis=-1, keepdims=True))
           - jnp.exp(jnp.sum(lq2_ref[...] * lk2_ref[...], axis=-1, keepdims=True)) + lambda_init)
    feat = lax.broadcasted_iota(jnp.int32, (LANES, tq), 0)
    kv_row = lax.broadcasted_iota(jnp.int32, (tq, 2 * tq), 0)
    q_col = lax.broadcasted_iota(jnp.int32, (tq, 2 * tq), 1)
    on_or_below_diag = kv_row <= jnp.where(q_col >= tq, q_col - tq, q_col)

    def chain(tiles):
        for qi in tiles:
            qsl = slice(qi * tq, (qi + 1) * tq)
            qt = qt_scr[:, qsl]
            zero = jnp.zeros_like(qt)
            qs = jnp.concatenate([jnp.where(feat < DA_HEAD_DIM, qt, zero),
                                  jnp.where(feat >= DA_HEAD_DIM, qt, zero)], axis=1)
            scores = lambda j: _dot(k_ref[j * tq:(j + 1) * tq, :], qs)
            m = jnp.full((1, 2 * tq), NEG, F32)
            acc = jnp.zeros((vt_scr.shape[0], 2 * tq), F32)
            s_next = scores(0)
            yield
            for j in range(qi + 1):
                s = s_next
                if j < qi:
                    s_next = scores(j + 1)
                    yield
                else:
                    s = jnp.where(on_or_below_diag, s, NEG)
                m_new = jnp.maximum(m, jnp.max(s, axis=0, keepdims=True))
                alpha = jnp.exp2(m - m_new)
                p = jnp.exp2(s - m_new)
                pv = _dot(vt_scr[:, j * tq:(j + 1) * tq], p.astype(BF16))
                yield
                acc = alpha * acc + pv
                m = m_new
            on = acc[0:LANES] * (1.0 / acc[LANES:LANES + 1])
            o = on[:, :tq] - lam * on[:, tq:]
            y = o * lax.rsqrt(jnp.mean(o * o, axis=0, keepdims=True) + EPS)
            o_ref[qsl, :] = (y.T * gain_ref[...] * (1.0 - lambda_init)).astype(o_ref.dtype)

    pairs = [[nq - 1 - i, i] for i in range(nq // 2)]
    for c0 in range(0, len(pairs), ATTN_CHAINS):
        _interleave(chain(tiles) for tiles in pairs[c0:c0 + ATTN_CHAINS])


def _diff_attn(proj3d, lq1, lk1, lq2, lk2, sub_gain, *, q_col, k_col, v_col, lambda_init, tq=256):
    b, t, _ = proj3d.shape
    vec = pl.BlockSpec((1, DA_HEAD_DIM), lambda bi, h: (0, 0))
    head = lambda col: pl.BlockSpec((None, t, LANES), lambda bi, h: (bi, 0, col + h))
    return pl.pallas_call(
        functools.partial(_attn_body, tq=tq, lambda_init=lambda_init),
        grid=(b, DA_HEADS),
        in_specs=[vec, vec, vec, vec, pl.BlockSpec((1, LANES), lambda bi, h: (0, 0)),
                  head(q_col), head(k_col), head(v_col)],
        out_specs=pl.BlockSpec((None, t, LANES), lambda bi, h: (bi, 0, h)),
        out_shape=jax.ShapeDtypeStruct((b, t, DA_HEADS * LANES), BF16),
        scratch_shapes=[pltpu.VMEM((LANES, t), BF16), pltpu.VMEM((LANES + BF16_ROWS, t), BF16)],
        compiler_params=pltpu.CompilerParams(
            dimension_semantics=("parallel", "parallel"), vmem_limit_bytes=VMEM_LIMIT),
        name="diff_attn",
    )(lq1, lk1, lq2, lk2, sub_gain, proj3d, proj3d, proj3d)


def _softplus(x):
    return jnp.maximum(x, 0.0) + jnp.log1p(jnp.exp(-jnp.abs(x)))


def _gdn_body(*refs):
    n_qkv = (2 * GDN_QK_HEADS + GDN_V_HEADS) * GDN_DIM // GDN_IN_BLOCK
    n_z = GDN_V_HEADS * GDN_DIM // GDN_IN_BLOCK
    qkv_refs, z_refs = refs[:n_qkv], refs[n_qkv:n_qkv + n_z]
    (ba_ref, convw_ref, prow_ref, onorm_ref, o_ref,
     xx_scr, state_scr, u_scr, wq_scr, kd_scr, intra_scr, eg_scr, out_scr) = refs[n_qkv + n_z:]
    slabs_per_block = GDN_IN_BLOCK // LANES
    c = CHUNK
    rows = ba_ref.shape[0]
    cis = list(range(rows // c))
    step = pl.program_id(1)
    wslot = step % 2
    rslot = 1 - wslot
    rep = GDN_V_HEADS // GDN_QK_HEADS
    hks = [(ci, hk) for ci in cis for hk in range(GDN_QK_HEADS)]
    hs = [(ci, h) for ci in cis for h in range(GDN_V_HEADS)]
    fh = lambda ci, h: ci * GDN_V_HEADS + h
    fk = lambda ci, hk: ci * GDN_QK_HEADS + hk
    bf = lambda y: y.astype(BF16)

    halo = SUBLANES

    def load_block(first_block):
        if first_block:
            xx_scr[:, 0:halo, :] = jnp.zeros((xx_scr.shape[0], halo, LANES), F32)
            state_scr[...] = jnp.zeros(state_scr.shape, F32)
        else:
            xx_scr[:, 0:halo, :] = xx_scr[:, rows:rows + halo, :]
        for slab in range(xx_scr.shape[0]):
            blk, off = divmod(slab, slabs_per_block)
            xx_scr[slab, halo:halo + rows, :] = (
                qkv_refs[blk][:, off * LANES:(off + 1) * LANES].astype(F32))

    def row_blocks():
        return [ROW_GROUP * g + r for g in range(c // ROW_GROUP) for r in range(ROW_STRIDE)]

    def perm_time(i):
        sub_bits, stride_bits = SUBLANES.bit_length() - 1, ROW_STRIDE.bit_length() - 1
        group = i >> (sub_bits + stride_bits)
        return ((group << (sub_bits + stride_bits)) + ((i & (SUBLANES - 1)) << stride_bits)
                + ((i >> sub_bits) & (ROW_STRIDE - 1)))

    def conv_silu(ci, slab):
        sl = slice(slab * LANES, (slab + 1) * LANES)
        w = [convw_ref[k:k + 1, sl] for k in range(CONV_WIDTH)]
        outs = []
        for g in range(c // ROW_GROUP):
            z = {rho: xx_scr[slab, pl.ds(halo + c * ci + ROW_GROUP * g + rho, SUBLANES,
                                         stride=ROW_STRIDE), :]
                 for rho in range(1 - CONV_WIDTH, ROW_STRIDE)}
            for r in range(ROW_STRIDE):
                y = w[CONV_WIDTH - 1] * z[r]
                for d in range(1, CONV_WIDTH):
                    y = y + w[CONV_WIDTH - 1 - d] * z[r - d]
                outs.append(y)
        y = jnp.concatenate(outs, axis=0)
        return y * jax.nn.sigmoid(y)

    def l2n(y):
        return y * lax.rsqrt(jnp.sum(y * y, axis=-1, keepdims=True) + EPS)

    lane = lax.broadcasted_iota(jnp.int32, (c, 2 * c), 1)
    first = lane < c
    zeros_c = jnp.zeros((c, 2 * c), BF16)

    def blockdiag(x):
        xb = bf(x)
        return jnp.concatenate([jnp.where(first, xb, zeros_c), jnp.where(first, zeros_c, xb)], axis=0)

    def recurrence():
        s = {h: state_scr[h] for h in range(GDN_V_HEADS)}
        o = {}
        for ci in cis:
            heads = range(GDN_V_HEADS)
            ws = {h: _dot(wq_scr[rslot, fh(ci, h)], bf(s[h])) for h in heads}
            yield
            vnb = {h: bf(u_scr[rslot, fh(ci, h)] - ws[h][:c]) for h in heads}
            zv = jnp.zeros((c, GDN_DIM), BF16)
            for hk in range(GDN_QK_HEADS):
                a_, b_ = rep * hk, rep * hk + 1
                vbd = jnp.concatenate([jnp.concatenate([vnb[a_], zv], axis=1),
                                       jnp.concatenate([zv, vnb[b_]], axis=1)], axis=0)
                ov = _dot(intra_scr[rslot, fk(ci, hk)], vbd)
                o[ci, a_] = ws[a_][c:] + ov[:, :GDN_DIM]
                o[ci, b_] = ws[b_][c:] + ov[:, GDN_DIM:]
            s = {h: s[h] * eg_scr[rslot, fh(ci, h)][0:1, :] + lax.dot_general(
                kd_scr[rslot, fh(ci, h)], vnb[h], TN_DIMS, preferred_element_type=F32) for h in heads}
            yield
        for h in range(GDN_V_HEADS):
            state_scr[h] = s[h]
        for ci, h in hs:
            y = _rms(o[ci, h], onorm_ref[...])
            for m, row0 in enumerate(row_blocks()):
                out_scr[fh(ci, h), pl.ds(row0, SUBLANES, stride=ROW_STRIDE), :] = (
                    y[SUBLANES * m:SUBLANES * (m + 1)])
        for ci, h in hs:
            blk, off = divmod(h, slabs_per_block)
            z = z_refs[blk][c * ci:c * (ci + 1), off * LANES:(off + 1) * LANES].astype(F32)
            o_ref[c * ci:c * (ci + 1), h * GDN_DIM:(h + 1) * GDN_DIM] = (
                out_scr[fh(ci, h)] * (z * jax.nn.sigmoid(z))).astype(o_ref.dtype)

    def prep(cis):
        hks = [(ci, hk) for ci in cis for hk in range(GDN_QK_HEADS)]
        hs = [(ci, h) for ci in cis for h in range(GDN_V_HEADS)]
        ii = perm_time(lax.broadcasted_iota(jnp.int32, (c, c), 0))
        jj = perm_time(lax.broadcasted_iota(jnp.int32, (c, c), 1))
        row2 = perm_time(lax.broadcasted_iota(jnp.int32, (c, 2 * c), 0))
        col2 = perm_time(jnp.where(first, lane, lane - c))
        strict = row2 > col2
        causal = row2 >= col2
        hp = lax.Precision.HIGHEST
        dup = lambda m: jnp.concatenate([m, m], axis=1)
        tril = jnp.where(ii >= jj, 1.0, 0.0).astype(F32)
        triu2 = dup(jnp.where(ii <= jj, 1.0, 0.0).astype(F32))
        eye2 = dup(jnp.where(ii == jj, 1.0, 0.0).astype(F32))
        gc_row, beta_row, gcol, egc, kdf = {}, {}, {}, {}, {}
        for ci in cis:
            ba = jnp.concatenate([ba_ref[pl.ds(c * ci + row0, SUBLANES, stride=ROW_STRIDE), :]
                                  for row0 in row_blocks()], axis=0)
            beta_col = jax.nn.sigmoid(ba)
            g_col = -jnp.exp(prow_ref[0:1, :]) * _softplus(ba + prow_ref[1:2, :])
            gc_col = jnp.dot(tril, g_col, precision=hp, preferred_element_type=F32)
            gc_row[ci] = lax.dot_general(g_col, triu2, TN_DIMS, precision=hp,
                                         preferred_element_type=F32)
            beta_row[ci] = lax.dot_general(beta_col, eye2, TN_DIMS, precision=hp,
                                           preferred_element_type=F32)
            g_last = gc_col[c - 1:c, :]
            egc_all = jnp.exp(gc_col)
            kdf_all = pltpu.roll(beta_col, GDN_V_HEADS, axis=1) * jnp.exp(g_last - gc_col)
            eg_all = jnp.exp(g_last)
            for h in range(GDN_V_HEADS):
                ln = slice(GDN_V_HEADS + h, GDN_V_HEADS + h + 1)
                gcol[ci, h] = jnp.broadcast_to(gc_col[:, ln], (c, 2 * c))
                egc[ci, h] = jnp.broadcast_to(egc_all[:, ln], (c, GDN_DIM))
                kdf[ci, h] = jnp.broadcast_to(kdf_all[:, ln], (c, GDN_DIM))
                eg_scr[wslot, fh(ci, h)] = jnp.broadcast_to(eg_all[:, ln], eg_scr.shape[2:])
        q = {(ci, hk): l2n(conv_silu(ci, hk)) * (GDN_DIM ** -0.5) for ci, hk in hks}
        k = {(ci, hk): l2n(conv_silu(ci, GDN_QK_HEADS + hk)) for ci, hk in hks}
        gram = {x: lax.dot_general(jnp.concatenate([bf(k[x]), bf(q[x])], axis=0),
                                   jnp.concatenate([bf(k[x]), bf(k[x])], axis=0),
                                   NT_DIMS, preferred_element_type=F32) for x in hks}
        yield
        pick = lambda m, hk: jnp.where(first[0:1, :], m[rep * hk:rep * hk + 1, :],
                                       m[rep * hk + 1:rep * hk + 2, :])
        a = {}
        for ci, hk in hks:
            gcol2 = jnp.where(first, gcol[ci, rep * hk], gcol[ci, rep * hk + 1])
            dec = jnp.where(causal, jnp.exp(gcol2 - pick(gc_row[ci][GDN_V_HEADS:], hk)), 0.0)
            brow = pick(beta_row[ci], hk)
            a[ci, hk] = jnp.where(strict, gram[ci, hk][:c] * dec, 0.0) * brow
            intra_scr[wslot, fk(ci, hk)] = bf(gram[ci, hk][c:] * dec * brow)
        n = {x: -a[x] for x in hks}
        p = {x: _dot(bf(a[x]), blockdiag(a[x])) for x in hks}
        yield
        for _ in range(c.bit_length() - 3):
            r_ = {x: _dot(jnp.concatenate([bf(n[x]), bf(p[x])], axis=0), blockdiag(p[x])) for x in hks}
            yield
            n = {x: n[x] + p[x] + r_[x][:c] for x in hks}
            p = {x: r_[x][c:] for x in hks}
        n = {x: n[x] + p[x] + _dot(bf(n[x]), blockdiag(p[x])) for x in hks}
        yield
        rhs = {(ci, h): jnp.concatenate([conv_silu(ci, 2 * GDN_QK_HEADS + h),
                                         k[ci, h // rep] * egc[ci, h]], axis=1) for ci, h in hs}
        zr = jnp.zeros((c, 2 * GDN_DIM), BF16)
        for ci, hk in hks:
            a_, b_ = rep * hk, rep * hk + 1
            rbd = jnp.concatenate([jnp.concatenate([bf(rhs[ci, a_]), zr], axis=1),
                                   jnp.concatenate([zr, bf(rhs[ci, b_])], axis=1)], axis=0)
            uw = (jnp.concatenate([rhs[ci, a_], rhs[ci, b_]], axis=1)
                  + _dot(bf(n[ci, hk]), rbd))
            for i_, h in enumerate((a_, b_)):
                o0 = 2 * GDN_DIM * i_
                u_scr[wslot, fh(ci, h)] = uw[:, o0:o0 + GDN_DIM]
                wq_scr[wslot, fh(ci, h)] = jnp.concatenate(
                    [bf(uw[:, o0 + GDN_DIM:o0 + 2 * GDN_DIM]), bf(q[ci, hk] * egc[ci, h])], axis=0)
                kd_scr[wslot, fh(ci, h)] = bf(k[ci, hk] * kdf[ci, h])

    last = pl.num_programs(1) - 1

    @pl.when(step == 0)
    def _():
        load_block(True)
        _interleave([prep([ci]) for ci in cis])

    @pl.when((step > 0) & (step < last))
    def _():
        load_block(False)
        _interleave([recurrence()] + [prep([ci]) for ci in cis])

    @pl.when(step == last)
    def _():
        _interleave([recurrence()])


def _gdn(proj3d, ba3d, conv_w, prow, onorm, *, qkv_col, z_col):
    b, t, _ = proj3d.shape
    c = CHUNK
    rows = c * GDN_CPB
    nt = t // rows
    conv_ch = conv_w.shape[1]
    v_width = GDN_V_HEADS * GDN_DIM
    cur = lambda bi, si: (bi, jnp.minimum(si, nt - 1))
    prev = lambda bi, si: (bi, jnp.maximum(si - 1, 0))
    const = lambda bi, si: (0, 0)
    heads = (2, GDN_CPB * GDN_V_HEADS)
    return pl.pallas_call(
        _gdn_body,
        grid=(b, nt + 1),
        in_specs=[
            *[pl.BlockSpec((None, rows, GDN_IN_BLOCK), lambda bi, si, j=j: cur(bi, si) + (qkv_col + j,))
              for j in range(conv_ch // GDN_IN_BLOCK)],
            *[pl.BlockSpec((None, rows, GDN_IN_BLOCK), lambda bi, si, j=j: prev(bi, si) + (z_col + j,))
              for j in range(v_width // GDN_IN_BLOCK)],
            pl.BlockSpec((None, rows, LANES), lambda bi, si: cur(bi, si) + (0,)),
            pl.BlockSpec((CONV_WIDTH, conv_ch), const),
            pl.BlockSpec((SUBLANES, LANES), const),
            pl.BlockSpec((1, GDN_DIM), const),
        ],
        out_specs=pl.BlockSpec((None, rows, v_width), lambda bi, si: prev(bi, si) + (0,)),
        out_shape=jax.ShapeDtypeStruct((b, t, v_width), BF16),
        scratch_shapes=[pltpu.VMEM((conv_ch // LANES, rows + SUBLANES, LANES), F32),
                        pltpu.VMEM((GDN_V_HEADS, GDN_DIM, GDN_DIM), F32),
                        pltpu.VMEM(heads + (c, GDN_DIM), F32),
                        pltpu.VMEM(heads + (2 * c, GDN_DIM), BF16),
                        pltpu.VMEM(heads + (c, GDN_DIM), BF16),
                        pltpu.VMEM((2, GDN_CPB * GDN_QK_HEADS, c, 2 * c), BF16),
                        pltpu.VMEM(heads + (SUBLANES, LANES), F32),
                        pltpu.VMEM(heads[1:] + (c, GDN_DIM), F32)],
        compiler_params=pltpu.CompilerParams(
            dimension_semantics=("parallel", "arbitrary"), vmem_limit_bytes=VMEM_LIMIT),
        name="gdn",
    )(*([proj3d] * ((conv_ch + v_width) // GDN_IN_BLOCK)), ba3d, conv_w, prow, onorm)


def _post_body(x_ref, oa_ref, ob_ref, ga_ref, gb_ref, p_ref, wa_ref, wb_ref, wo_ref, g_mix_ref,
               g_pre_ref, wup_ref, wdn_ref, g_post_ref, wple_ref, wgate_ref, g_ple_ref, out_ref,
               *, ff_chunk):
    ya = _dot(oa_ref[...], wa_ref[...])
    yb = _dot(ob_ref[...], wb_ref[...])
    merged = (jax.nn.sigmoid(ga_ref[...].astype(F32)) * ya
              + jax.nn.sigmoid(gb_ref[...].astype(F32)) * yb)
    h = x_ref[...] + _rms(_dot(merged.astype(BF16), wo_ref[...]), g_mix_ref[...])
    u = _rms(h, g_pre_ref[...]).astype(BF16)
    d_ff = wup_ref.shape[1]
    acc = jnp.zeros(h.shape, F32)
    for c0 in range(0, d_ff, ff_chunk):
        hid = jnp.square(jnp.maximum(_dot(u, wup_ref[:, c0:c0 + ff_chunk]), 0.0))
        acc = acc + _dot(hid.astype(BF16), wdn_ref[c0:c0 + ff_chunk, :])
    h = h + _rms(acc, g_post_ref[...])
    e = _dot(p_ref[...].astype(BF16), wple_ref[...]) * jax.nn.sigmoid(
        _dot(h.astype(BF16), wgate_ref[...]))
    out_ref[...] = h + _rms(e, g_ple_ref[...])


def _post(x2d, o_a, o_b, proj, p2d, w_a, w_b, w_o, g_mix, g_pre, w_up, w_dn, g_post, w_ple, w_gate,
          g_ple, *, ga_col, gb_col, tm=512, ff_chunk=1024):
    n, d = x2d.shape
    row = lambda i: (i, 0)
    const = lambda i: (0, 0)
    weight = lambda w: pl.BlockSpec(w.shape, const, pipeline_mode=pl.Buffered(1))
    gain = pl.BlockSpec((1, d), const)
    return pl.pallas_call(
        functools.partial(_post_body, ff_chunk=ff_chunk),
        grid=(n // tm,),
        in_specs=[
            pl.BlockSpec((tm, d), row),
            pl.BlockSpec((tm, o_a.shape[1]), row),
            pl.BlockSpec((tm, o_b.shape[1]), row),
            pl.BlockSpec((tm, d), lambda i: (i, ga_col)),
            pl.BlockSpec((tm, d), lambda i: (i, gb_col)),
            pl.BlockSpec((tm, p2d.shape[1]), row),
            weight(w_a), weight(w_b), weight(w_o), gain,
            gain, weight(w_up), weight(w_dn), gain,
            weight(w_ple), weight(w_gate), gain,
        ],
        out_specs=pl.BlockSpec((tm, d), row),
        out_shape=jax.ShapeDtypeStruct((n, d), F32),
        compiler_params=pltpu.CompilerParams(
            dimension_semantics=("parallel",), vmem_limit_bytes=VMEM_LIMIT),
        name="post",
    )(x2d, o_a, o_b, proj, proj, p2d, w_a, w_b, w_o, g_mix, g_pre, w_up, w_dn, g_post, w_ple,
      w_gate, g_ple)


def _layer(h, p_i, i, pre_mix_norm, w_in, conv_w, lambda_q1, lambda_k1, lambda_q2, lambda_k2,
           da_sub_norm, gdn_a_log, gdn_dt_bias, gdn_out_norm, w_branch_a, w_branch_b, w_out,
           post_mix_norm, pre_mlp_norm, w_up, w_down, post_mlp_norm, w_ple, w_ple_gate, ple_norm):
    b, t, d = h.shape
    n = b * t
    da_w = DA_HEADS * 2 * DA_HEAD_DIM
    gqk_w = GDN_QK_HEADS * GDN_DIM
    gv_w = GDN_V_HEADS * GDN_DIM
    conv_ch = 2 * gqk_w + gv_w
    sizes = (da_w, da_w, da_w, conv_ch, gv_w, GDN_V_HEADS, GDN_V_HEADS, d, d)
    offs = [0]
    for s in sizes:
        offs.append(offs[-1] + s)
    row_scale = jnp.ones((offs[-1], 1), F32).at[:da_w].set(math.log2(math.e))
    w_t = (w_in.T * row_scale).astype(BF16)
    n_head = offs[5]
    w_gate_t = w_t[offs[7]:]
    w_ba_t = jnp.pad(w_t[offs[5]:offs[7]], ((0, LANES - 2 * GDN_V_HEADS), (0, 0)))
    col = {"q": 0, "k": da_w, "v": 2 * da_w, "conv": 3 * da_w, "z": 3 * da_w + conv_ch,
           "ga": n_head, "gb": n_head + d}

    x2d = h.reshape(n, d)
    proj, ba = _in_proj(x2d, pre_mix_norm.reshape(1, d), w_t, w_gate_t, w_ba_t, n_head=n_head)
    proj3d = proj.reshape(b, t, -1)

    lambda_init = 0.8 - 0.6 * math.exp(-0.3 * i)
    o_a = _diff_attn(proj3d, lambda_q1.reshape(1, -1), lambda_k1.reshape(1, -1),
                     lambda_q2.reshape(1, -1), lambda_k2.reshape(1, -1), da_sub_norm.reshape(1, -1),
                     q_col=col["q"] // LANES, k_col=col["k"] // LANES, v_col=col["v"] // LANES,
                     lambda_init=lambda_init)

    lanes = jnp.zeros((LANES,), F32)
    a_log = lanes.at[GDN_V_HEADS:2 * GDN_V_HEADS].set(gdn_a_log.astype(F32))
    dt_b = lanes.at[GDN_V_HEADS:2 * GDN_V_HEADS].set(gdn_dt_bias.astype(F32))
    prow = jnp.zeros((SUBLANES, LANES), F32).at[0].set(a_log).at[1].set(dt_b)
    o_b = _gdn(proj3d, ba.reshape(b, t, LANES), conv_w, prow, gdn_out_norm.reshape(1, -1),
               qkv_col=col["conv"] // GDN_IN_BLOCK, z_col=col["z"] // GDN_IN_BLOCK)

    out = _post(x2d, o_a.reshape(n, -1), o_b.reshape(n, -1), proj, p_i.reshape(n, -1),
                w_branch_a.astype(BF16), w_branch_b.astype(BF16), w_out.astype(BF16),
                post_mix_norm.reshape(1, d), pre_mlp_norm.reshape(1, d), w_up.astype(BF16),
                w_down.astype(BF16), post_mlp_norm.reshape(1, d), w_ple.astype(BF16),
                w_ple_gate.astype(BF16), ple_norm.reshape(1, d),
                ga_col=col["ga"] // d, gb_col=col["gb"] // d)
    return out.reshape(b, t, d)


def kernel(x, p, pre_mix_norm, w_in, conv_w, lambda_q1, lambda_k1, lambda_q2, lambda_k2,
           da_sub_norm, gdn_a_log, gdn_dt_bias, gdn_out_norm, w_branch_a, w_branch_b, w_out,
           post_mix_norm, pre_mlp_norm, w_up, w_down, post_mlp_norm, w_ple, w_ple_gate, ple_norm):
    per_layer = (pre_mix_norm, w_in, conv_w, lambda_q1, lambda_k1, lambda_q2, lambda_k2,
                 da_sub_norm, gdn_a_log, gdn_dt_bias, gdn_out_norm, w_branch_a, w_branch_b, w_out,
                 post_mix_norm, pre_mlp_norm, w_up, w_down, post_mlp_norm, w_ple, w_ple_gate,
                 ple_norm)
    h = x
    for i in range(p.shape[0]):
        h = _layer(h, p[i], i, *(w[i] for w in per_layer))
    return h
```
